```python
import math
import jax
import jax.numpy as jnp
from jax import lax
import numpy as np

D_MODEL = 1024
BATCH = 4
SEQ = 4096
DEPTH = 4
DEC_BATCH = 128
DEC_SEQ = 1
PAST_LEN = 8192
PAGE_SIZE = 128

N_MIXERS = 4
N_L_CONF = (DEPTH + 3) // N_MIXERS
N_L_SWA = (DEPTH + 2) // N_MIXERS
N_L_GDN = (DEPTH + 1) // N_MIXERS
N_L_SCONV = DEPTH // N_MIXERS
D_FF = 4 * D_MODEL
CONF_K = 31
HEAD_DIM = 64
N_HEADS = D_MODEL // HEAD_DIM
N_KV = 4
GQA_GROUP = N_HEADS // N_KV
WINDOW = 128
ROT_DIM = HEAD_DIM // 4
ROPE_THETA = 500000.0
GDN_DK = 128
GDN_DV = 128
GDN_HK = D_MODEL // GDN_DK
GDN_HV = 2 * GDN_HK
GDN_QK_W = GDN_HK * GDN_DK
GDN_V_W = GDN_HV * GDN_DV
GDN_CONV_DIM = 2 * GDN_QK_W + GDN_V_W
GDN_IN_W = GDN_CONV_DIM + GDN_V_W + 2 * GDN_HV
GDN_CONV_K = 4
GDN_CHUNK = 64
SCONV_K = 3
RMS_EPS = 1e-6
LN_EPS = 1e-5
L2_EPS = 1e-6

kernel_name = 'hybrid_conformer_swa_gdn_shortconv_adaln_step'


def rmsnorm(x, g):
    xf = x.astype(jnp.float32)
    y = xf * lax.rsqrt(jnp.mean(xf * xf, axis=-1, keepdims=True) + RMS_EPS)
    return (y * g.astype(jnp.float32)).astype(x.dtype)


def modulate(h, shift, scale):
    return h * (1 + scale[:, None, :]) + shift[:, None, :]


def l2norm(x):
    return x * lax.rsqrt(jnp.sum(x * x, axis=-1, keepdims=True) + L2_EPS)


def partial_rope(x, pos):
    half = ROT_DIM // 2
    inv = jnp.power(jnp.float32(ROPE_THETA), -jnp.arange(half, dtype=jnp.float32) * 2.0 / ROT_DIM)
    ang = pos.astype(jnp.float32)[:, None] * inv[None, :]
    cos = jnp.cos(ang)[None, :, None, :]
    sin = jnp.sin(ang)[None, :, None, :]
    xf = x.astype(jnp.float32)
    x1, x2 = xf[..., :half], xf[..., half:ROT_DIM]
    out = jnp.concatenate([x1 * cos - x2 * sin, x2 * cos + x1 * sin, xf[..., ROT_DIM:]], axis=-1)
    return out.astype(x.dtype)


def causal_dwconv(u, buf, w):
    k = w.shape[0]
    full = jnp.concatenate([buf.astype(u.dtype), u], axis=1)
    y = lax.conv_general_dilated(full, w.astype(u.dtype)[:, None, :], window_strides=(1,),
                                 padding='VALID', dimension_numbers=('NWC', 'WIO', 'NWC'),
                                 feature_group_count=u.shape[-1])
    return y, full[:, full.shape[1] - (k - 1):]


def conformer_conv(h, buf, w_pw1, b_pw1, w_dw, b_dw, ln_g, ln_b, w_pw2, b_pw2):
    a, gate = jnp.split(h @ w_pw1 + b_pw1, 2, axis=-1)
    u = a * jax.nn.sigmoid(gate)
    y, new_buf = causal_dwconv(u, buf, w_dw)
    yf = (y + b_dw).astype(jnp.float32)
    mu = jnp.mean(yf, axis=-1, keepdims=True)
    var = jnp.mean(jnp.square(yf - mu), axis=-1, keepdims=True)
    yn = ((yf - mu) * lax.rsqrt(var + LN_EPS) * ln_g.astype(jnp.float32)
          + ln_b.astype(jnp.float32)).astype(h.dtype)
    return jax.nn.silu(yn) @ w_pw2 + b_pw2, new_buf


def sink_attention(q, k, v, mask, sinks):
    s = jnp.einsum('bnqkgd,bnskd->bnkgqs', q, k).astype(jnp.float32) * (HEAD_DIM ** -0.5)
    s = jnp.where(mask[None, :, None, None], s, -jnp.inf)
    sink = sinks.astype(jnp.float32)[None, None, :, :, None, None]
    m = jnp.maximum(jnp.max(s, axis=-1, keepdims=True), sink)
    p = jnp.exp(s - m)
    p = p / (jnp.sum(p, axis=-1, keepdims=True) + jnp.exp(sink - m))
    return jnp.einsum('bnkgqs,bnskd->bnqkgd', p.astype(v.dtype), v)


def swa_mixer(h, pos, kbuf, vbuf, w_qkv, w_o, sinks, prompt):
    B, T, _ = h.shape
    q, k, v = jnp.split(h @ w_qkv, [N_HEADS * HEAD_DIM, (N_HEADS + N_KV) * HEAD_DIM], axis=-1)
    q = partial_rope(q.reshape(B, T, N_HEADS, HEAD_DIM), pos)
    k = partial_rope(k.reshape(B, T, N_KV, HEAD_DIM), pos)
    v = v.reshape(B, T, N_KV, HEAD_DIM)
    kk = jnp.concatenate([kbuf.astype(k.dtype), k], axis=1)
    vv = jnp.concatenate([vbuf.astype(v.dtype), v], axis=1)
    sink = sinks.reshape(N_KV, GQA_GROUP)
    if prompt:
        nb = T // WINDOW
        kb = kk.reshape(B, nb + 1, WINDOW, N_KV, HEAD_DIM)
        vb = vv.reshape(B, nb + 1, WINDOW, N_KV, HEAD_DIM)
        kb = jnp.concatenate([kb[:, :-1], kb[:, 1:]], axis=2)
        vb = jnp.concatenate([vb[:, :-1], vb[:, 1:]], axis=2)
        qb = q.reshape(B, nb, WINDOW, N_KV, GQA_GROUP, HEAD_DIM)
        qi = jnp.arange(WINDOW)[:, None]
        kj = jnp.arange(2 * WINDOW)[None, :]
        band = (kj >= qi) & (kj <= qi + WINDOW)
        key_ok = (jnp.arange(nb)[:, None] * WINDOW + kj) >= WINDOW
        mask = band[None] & key_ok[:, None, :]
    else:
        kb, vb = kk[:, None], vv[:, None]
        qb = q.reshape(B, 1, T, N_KV, GQA_GROUP, HEAD_DIM)
        qi = jnp.arange(T)[:, None]
        kj = jnp.arange(WINDOW + T)[None, :]
        mask = ((kj >= qi) & (kj <= qi + WINDOW))[None]
    o = sink_attention(qb, kb, vb, mask, sink).reshape(B, T, N_HEADS * HEAD_DIM)
    return o @ w_o, kk[:, T:], vv[:, T:]


def gdn_chunked(q, k, v, g, beta, S0):
    B, T, H, _ = q.shape
    DV = v.shape[-1]
    C = GDN_CHUNK
    N = T // C

    def blk(a):
        return jnp.moveaxis(a.reshape((B, N, C, H) + a.shape[3:]), 3, 2)

    q, k, v, g, beta = blk(q), blk(k), blk(v), blk(g), blk(beta)
    gc = jnp.cumsum(g, axis=-1)
    lower = jnp.tril(jnp.ones((C, C), dtype=bool))
    strict = jnp.tril(jnp.ones((C, C), dtype=bool), -1)
    decay = jnp.exp(jnp.where(lower, gc[..., :, None] - gc[..., None, :], -jnp.inf))
    kb = k * beta[..., None]
    vb = v * beta[..., None]
    L = jnp.where(strict, jnp.einsum('bnhcd,bnhsd->bnhcs', kb, k) * decay, 0.0)
    eye = jnp.eye(C, dtype=jnp.float32)
    A = L + eye
    Tm = lax.linalg.triangular_solve(A, jnp.broadcast_to(eye, A.shape), left_side=True, lower=True)
    u = Tm @ vb
    w = Tm @ (kb * jnp.exp(gc)[..., None])
    attn = jnp.where(lower, jnp.einsum('bnhcd,bnhsd->bnhcs', q, k) * decay, 0.0)

    def step(S, xs):
        q_n, k_n, u_n, w_n, gc_n, attn_n = xs
        v_new = u_n - w_n @ S
        o = (q_n * jnp.exp(gc_n)[..., None]) @ S + attn_n @ v_new
        g_last = gc_n[..., -1]
        S = S * jnp.exp(g_last)[..., None, None] + jnp.einsum(
            'bhck,bhcv->bhkv', k_n * jnp.exp(g_last[..., None] - gc_n)[..., None], v_new)
        return S, o

    xs = tuple(jnp.moveaxis(a, 1, 0) for a in (q, k, u, w, gc, attn))
    S, o = lax.scan(step, S0, xs)
    o = jnp.moveaxis(jnp.moveaxis(o, 0, 1), 3, 2).reshape(B, T, H, DV)
    return o, S


def gdn_recurrent(q, k, v, g, beta, S0):
    def step(S, xs):
        q_t, k_t, v_t, g_t, b_t = xs
        S = S * jnp.exp(g_t)[..., None, None]
        kv = jnp.einsum('bhk,bhkv->bhv', k_t, S)
        S = S + jnp.einsum('bhk,bhv->bhkv', k_t, (v_t - kv) * b_t[..., None])
        return S, jnp.einsum('bhk,bhkv->bhv', q_t, S)

    xs = tuple(jnp.swapaxes(a, 0, 1) for a in (q, k, v, g, beta))
    S, o = lax.scan(step, S0, xs)
    return jnp.swapaxes(o, 0, 1), S


def gdn_mixer(h, S0, cbuf, w_in, w_conv, a_log, dt_bias, norm_g, w_o, prompt):
    B, T, _ = h.shape
    f32 = jnp.float32
    qkv, z, b, a = jnp.split(h @ w_in, [GDN_CONV_DIM, GDN_CONV_DIM + GDN_V_W,
                                        GDN_CONV_DIM + GDN_V_W + GDN_HV], axis=-1)
    qkv, new_cbuf = causal_dwconv(qkv, cbuf, w_conv)
    qkv = jax.nn.silu(qkv).astype(f32)
    q, k, v = jnp.split(qkv, [GDN_QK_W, 2 * GDN_QK_W], axis=-1)
    rep = GDN_HV // GDN_HK
    q = jnp.repeat(l2norm(q.reshape(B, T, GDN_HK, GDN_DK)), rep, axis=2) * (GDN_DK ** -0.5)
    k = jnp.repeat(l2norm(k.reshape(B, T, GDN_HK, GDN_DK)), rep, axis=2)
    v = v.reshape(B, T, GDN_HV, GDN_DV)
    beta = jax.nn.sigmoid(b.astype(f32))
    g = -jnp.exp(a_log.astype(f32)) * jax.nn.softplus(a.astype(f32) + dt_bias.astype(f32))
    if prompt:
        o, S = gdn_chunked(q, k, v, g, beta, S0.astype(f32))
    else:
        o, S = gdn_recurrent(q, k, v, g, beta, S0.astype(f32))
    o = rmsnorm(o, norm_g) * jax.nn.silu(z.reshape(B, T, GDN_HV, GDN_DV).astype(f32))
    o = o.astype(h.dtype).reshape(B, T, GDN_V_W)
    return o @ w_o, S.astype(h.dtype), new_cbuf


def sconv_mixer(h, buf, w_in, w_conv, w_out):
    gate_b, gate_c, hin = jnp.split(h @ w_in, 3, axis=-1)
    y, new_buf = causal_dwconv(gate_c * hin, buf, w_conv)
    return (gate_b * y) @ w_out, new_buf


def sq_relu_mlp(h, w_up, w_down):
    return jnp.square(jax.nn.relu(h @ w_up)) @ w_down


def trunk(x, c, pos, st, P, prompt):
    new = {'conf': [], 'k': [], 'v': [], 'ssm': [], 'gconv': [], 'sconv': []}
    cm = jax.nn.silu(c)
    for i in range(DEPTH):
        m, j = i % N_MIXERS, i // N_MIXERS
        sh1, sc1, g1, sh2, sc2, g2 = jnp.split(cm @ P['w_ada'][i] + P['b_ada'][i], 6, axis=-1)
        h = modulate(rmsnorm(x, P['norm_mix'][i]), sh1, sc1)
        if m == 0:
            out, nb = conformer_conv(h, st['conf'][j], P['conf_w_pw1'][j], P['conf_b_pw1'][j],
                                     P['conf_w_dw'][j], P['conf_b_dw'][j], P['conf_ln_g'][j],
                                     P['conf_ln_b'][j], P['conf_w_pw2'][j], P['conf_b_pw2'][j])
            new['conf'].append(nb)
        elif m == 1:
            out, nk, nv = swa_mixer(h, pos, st['k'][j], st['v'][j], P['swa_w_qkv'][j],
                                    P['swa_w_o'][j], P['swa_sinks'][j], prompt)
            new['k'].append(nk)
            new['v'].append(nv)
        elif m == 2:
            out, ns, nc = gdn_mixer(h, st['ssm'][j], st['gconv'][j], P['gdn_w_in'][j],
                                    P['gdn_w_conv'][j], P['gdn_a_log'][j], P['gdn_dt_bias'][j],
                                    P['gdn_norm'][j], P['gdn_w_o'][j], prompt)
            new['ssm'].append(ns)
            new['gconv'].append(nc)
        else:
            out, nb = sconv_mixer(h, st['sconv'][j], P['sconv_w_in'][j], P['sconv_w_conv'][j],
                                  P['sconv_w_out'][j])
            new['sconv'].append(nb)
        x = x + g1[:, None, :] * out
        h = modulate(rmsnorm(x, P['norm_mlp'][i]), sh2, sc2)
        x = x + g2[:, None, :] * sq_relu_mlp(h, P['w_up'][i], P['w_down'][i])
    stacked = {name: jnp.stack(rows) for name, rows in new.items()}
    return rmsnorm(x, P['norm_final']), stacked


def setup_inputs(seed: int = 0) -> dict:
    key = jax.random.key(seed)
    ks = list(jax.random.split(key, 48))
    f32 = jnp.float32
    d = D_MODEL

    def nrm(shape, scale):
        return jax.random.normal(ks.pop(), shape, f32) * scale

    def gain(shape):
        return 1.0 + nrm(shape, 0.02)

    a_log = jnp.log(jax.random.uniform(ks.pop(), (N_L_GDN, GDN_HV), f32, 1.0, 16.0))
    dt = jnp.exp(jax.random.uniform(ks.pop(), (N_L_GDN, GDN_HV), f32,
                                    math.log(1e-3), math.log(1e-1)))
    dt_bias = dt + jnp.log(-jnp.expm1(-dt))
    return {
        'x_prompt': nrm((BATCH, SEQ, d), 1.0),
        'x_sample': nrm((DEC_BATCH, DEC_SEQ, d), 1.0),
        'c_prompt': nrm((BATCH, d), 1.0),
        'c_sample': nrm((DEC_BATCH, d), 1.0),
        'state_conf_conv': nrm((N_L_CONF, DEC_BATCH, CONF_K - 1, d), 0.5),
        'cache_swa_k': nrm((N_L_SWA, DEC_BATCH, WINDOW, N_KV, HEAD_DIM), 1.0),
        'cache_swa_v': nrm((N_L_SWA, DEC_BATCH, WINDOW, N_KV, HEAD_DIM), 1.0),
        'state_gdn_ssm': nrm((N_L_GDN, DEC_BATCH, GDN_HV, GDN_DK, GDN_DV), 0.1),
        'state_gdn_conv': nrm((N_L_GDN, DEC_BATCH, GDN_CONV_K - 1, GDN_CONV_DIM), 1.0),
        'state_sconv': nrm((N_L_SCONV, DEC_BATCH, SCONV_K - 1, d), 0.5),
        'w_ada': nrm((DEPTH, d, 6 * d), 0.5 * d ** -0.5),
        'b_ada': nrm((DEPTH, 6 * d), 0.02),
        'norm_mix': gain((DEPTH, d)),
        'norm_mlp': gain((DEPTH, d)),
        'w_up': nrm((DEPTH, d, D_FF), d ** -0.5),
        'w_down': nrm((DEPTH, D_FF, d), D_FF ** -0.5),
        'norm_final': gain((d,)),
        'conf_w_pw1': nrm((N_L_CONF, d, 2 * d), d ** -0.5),
        'conf_b_pw1': nrm((N_L_CONF, 2 * d), 0.02),
        'conf_w_dw': nrm((N_L_CONF, CONF_K, d), CONF_K ** -0.5),
        'conf_b_dw': nrm((N_L_CONF, d), 0.02),
        'conf_ln_g': gain((N_L_CONF, d)),
        'conf_ln_b': nrm((N_L_CONF, d), 0.02),
        'conf_w_pw2': nrm((N_L_CONF, d, d), d ** -0.5),
        'conf_b_pw2': nrm((N_L_CONF, d), 0.02),
        'swa_w_qkv': nrm((N_L_SWA, d, (N_HEADS + 2 * N_KV) * HEAD_DIM), d ** -0.5),
        'swa_w_o': nrm((N_L_SWA, N_HEADS * HEAD_DIM, d), (N_HEADS * HEAD_DIM) ** -0.5),
        'swa_sinks': nrm((N_L_SWA, N_HEADS), 0.5),
        'gdn_w_in': nrm((N_L_GDN, d, GDN_IN_W), d ** -0.5),
        'gdn_w_conv': nrm((N_L_GDN, GDN_CONV_K, GDN_CONV_DIM), GDN_CONV_K ** -0.5),
        'gdn_a_log': a_log,
        'gdn_dt_bias': dt_bias,
        'gdn_norm': gain((N_L_GDN, GDN_DV)),
        'gdn_w_o': nrm((N_L_GDN, GDN_V_W, d), GDN_V_W ** -0.5),
        'sconv_w_in': nrm((N_L_SCONV, d, 3 * d), d ** -0.5),
        'sconv_w_conv': nrm((N_L_SCONV, SCONV_K, d), SCONV_K ** -0.5),
        'sconv_w_out': nrm((N_L_SCONV, d, d), d ** -0.5),
    }


def reference(x_prompt, x_sample, c_prompt, c_sample, state_conf_conv, cache_swa_k, cache_swa_v,
              state_gdn_ssm, state_gdn_conv, state_sconv, w_ada, b_ada, norm_mix, norm_mlp,
              w_up, w_down, norm_final, conf_w_pw1, conf_b_pw1, conf_w_dw, conf_b_dw, conf_ln_g,
              conf_ln_b, conf_w_pw2, conf_b_pw2, swa_w_qkv, swa_w_o, swa_sinks, gdn_w_in,
              gdn_w_conv, gdn_a_log, gdn_dt_bias, gdn_norm, gdn_w_o, sconv_w_in, sconv_w_conv,
              sconv_w_out):
    P = {'w_ada': w_ada, 'b_ada': b_ada, 'norm_mix': norm_mix, 'norm_mlp': norm_mlp,
         'w_up': w_up, 'w_down': w_down, 'norm_final': norm_final,
         'conf_w_pw1': conf_w_pw1, 'conf_b_pw1': conf_b_pw1, 'conf_w_dw': conf_w_dw,
         'conf_b_dw': conf_b_dw, 'conf_ln_g': conf_ln_g, 'conf_ln_b': conf_ln_b,
         'conf_w_pw2': conf_w_pw2, 'conf_b_pw2': conf_b_pw2,
         'swa_w_qkv': swa_w_qkv, 'swa_w_o': swa_w_o, 'swa_sinks': swa_sinks,
         'gdn_w_in': gdn_w_in, 'gdn_w_conv': gdn_w_conv, 'gdn_a_log': gdn_a_log,
         'gdn_dt_bias': gdn_dt_bias, 'gdn_norm': gdn_norm, 'gdn_w_o': gdn_w_o,
         'sconv_w_in': sconv_w_in, 'sconv_w_conv': sconv_w_conv, 'sconv_w_out': sconv_w_out}
    bp, dt = x_prompt.shape[0], x_prompt.dtype
    prompt_state = {
        'conf': jnp.zeros((N_L_CONF, bp, CONF_K - 1, D_MODEL), dt),
        'k': jnp.zeros((N_L_SWA, bp, WINDOW, N_KV, HEAD_DIM), dt),
        'v': jnp.zeros((N_L_SWA, bp, WINDOW, N_KV, HEAD_DIM), dt),
        'ssm': jnp.zeros((N_L_GDN, bp, GDN_HV, GDN_DK, GDN_DV), dt),
        'gconv': jnp.zeros((N_L_GDN, bp, GDN_CONV_K - 1, GDN_CONV_DIM), dt),
        'sconv': jnp.zeros((N_L_SCONV, bp, SCONV_K - 1, D_MODEL), dt)}
    sample_state = {'conf': state_conf_conv, 'k': cache_swa_k, 'v': cache_swa_v,
                    'ssm': state_gdn_ssm, 'gconv': state_gdn_conv, 'sconv': state_sconv}
    pos_p = jnp.arange(x_prompt.shape[1], dtype=jnp.int32)
    pos_s = PAST_LEN + jnp.arange(x_sample.shape[1], dtype=jnp.int32)
    y_prompt, sp = trunk(x_prompt, c_prompt, pos_p, prompt_state, P, True)
    y_sample, ss = trunk(x_sample, c_sample, pos_s, sample_state, P, False)
    return (y_prompt, y_sample, sp['conf'], ss['conf'], sp['k'], ss['k'], sp['v'], ss['v'],
            sp['ssm'], ss['ssm'], sp['gconv'], ss['gconv'], sp['sconv'], ss['sconv'])
```

```python
import functools
import math

import jax
import jax.numpy as jnp
from jax import lax
from jax.experimental import pallas as pl
from jax.experimental.pallas import tpu as pltpu

F32 = jnp.float32
BF16 = jnp.bfloat16

D_MODEL = 1024
DEPTH = 4
D_FF = 4 * D_MODEL
CONF_K = 31
HEAD_DIM = 64
N_HEADS = 16
N_KV = 4
WINDOW = 128
ROT_DIM = 16
ROPE_THETA = 500000.0
PAST_LEN = 8192
GDN_DK = 128
GDN_DV = 128
GDN_HK = 8
GDN_HV = 16
GDN_QK_W = GDN_HK * GDN_DK
GDN_V_W = GDN_HV * GDN_DV
GDN_CONV_DIM = 2 * GDN_QK_W + GDN_V_W
GDN_CONV_K = 4
GDN_CHUNK = 64
SCONV_K = 3
RMS_EPS = 1e-6
LN_EPS = 1e-5
L2_EPS = 1e-6

LANES = 128
SUBLANES = 8
VMEM_LIMIT = 56 * 1024 * 1024


def _cparams(sem):
    return pltpu.CompilerParams(dimension_semantics=sem, vmem_limit_bytes=VMEM_LIMIT)


def _rows2d(ref):
    return ref[0] if len(ref.shape) == 3 else ref[...]


def _mod_spec(arr, tm, rows_per_seq):
    d = arr.shape[-1]
    if arr.ndim == 3:
        return pl.BlockSpec((1, 1, d), lambda i, *_: ((i * tm) // rows_per_seq, 0, 0))
    return pl.BlockSpec((tm, d), lambda i, *_: (i, 0))


def _const_spec(arr):
    nd = arr.ndim
    return pl.BlockSpec(arr.shape, lambda *_: (0,) * nd)


def _normmod(x, g, sh, sc):
    y = x * lax.rsqrt(jnp.mean(x * x, axis=-1, keepdims=True) + RMS_EPS)
    y = y * g
    return y * (1.0 + sc) + sh


def _sigmoid(x):
    return jax.nn.sigmoid(x)


def _silu(x):
    return x * jax.nn.sigmoid(x)


def _dot(a, b):
    return jnp.dot(a.astype(BF16), b.astype(BF16), preferred_element_type=F32)


def _ada_kernel(c_ref, w_ref, b_ref, o_ref):
    cm = _silu(c_ref[...])
    o_ref[0] = _dot(cm, w_ref[0]) + b_ref[0]


def ada_all(c, w_ada, b_ada, *, tn=1024):
    depth, d, n = w_ada.shape
    r = c.shape[0]
    return pl.pallas_call(
        _ada_kernel,
        out_shape=jax.ShapeDtypeStruct((depth, r, n), F32),
        grid=(depth, n // tn),
        in_specs=[pl.BlockSpec((r, d), lambda l, j: (0, 0)),
                  pl.BlockSpec((1, d, tn), lambda l, j: (l, 0, j)),
                  pl.BlockSpec((1, 1, tn), lambda l, j: (l, 0, j))],
        out_specs=pl.BlockSpec((1, r, tn), lambda l, j: (l, 0, j)),
        compiler_params=_cparams(("parallel", "parallel")),
        name="ada_all",
    )(c, w_ada, b_ada.reshape(depth, 1, n))


def _nmm_kernel(*refs, splits, has_bias, n_extra, n_out, epilogue, side_fn, n_side_in,
                n_side_out):
    x_ref, g_ref, sh_ref, sc_ref = refs[:4]
    p = 4
    w_refs = refs[p:p + splits]; p += splits
    b_refs = refs[p:p + splits] if has_bias else (); p += splits if has_bias else 0
    e_refs = refs[p:p + n_extra]; p += n_extra
    si_refs = refs[p:p + n_side_in]; p += n_side_in
    o_refs = refs[p:p + n_out]; p += n_out
    so_refs = refs[p:p + n_side_out]; p += n_side_out
    h_scr = refs[p]
    j = pl.program_id(1)

    @pl.when(j == 0)
    def _():
        h = _normmod(x_ref[...], g_ref[...], _rows2d(sh_ref), _rows2d(sc_ref))
        h_scr[...] = h.astype(BF16)
        if side_fn is not None:
            side_fn(h_scr[...], si_refs, so_refs)

    h = h_scr[...]
    accs = []
    for s in range(splits):
        a = jnp.dot(h, w_refs[s][...], preferred_element_type=F32)
        if has_bias:
            a = a + b_refs[s][...]
        accs.append(a)
    outs = epilogue(accs, e_refs, j)
    for o_ref, o in zip(o_refs, outs):
        o_ref[...] = o.astype(o_ref.dtype)


def nmm(x, gnorm, sh, sc, w, bias, *, splits, n_out, epilogue, tm, tn, rows_per_seq,
        extras=(), extra_specs=(), side=None, name):
    m, d = x.shape
    side_fn, side_in, side_in_specs, side_out_shapes, side_out_specs = (
        side if side is not None else (None, (), (), (), ()))
    n = w.shape[1]
    ng = n // splits
    nj = ng // tn
    in_specs = [pl.BlockSpec((tm, d), lambda i, j: (i, 0)),
                pl.BlockSpec((1, d), lambda i, j: (0, 0)),
                _mod_spec(sh, tm, rows_per_seq), _mod_spec(sc, tm, rows_per_seq)]
    args = [x, gnorm.reshape(1, d), sh, sc]
    for s in range(splits):
        in_specs.append(pl.BlockSpec((d, tn), lambda i, j, s=s: (0, s * nj + j)))
        args.append(w)
    if bias is not None:
        b2 = bias.reshape(1, n)
        for s in range(splits):
            in_specs.append(pl.BlockSpec((1, tn), lambda i, j, s=s: (0, s * nj + j)))
            args.append(b2)
    in_specs += list(extra_specs) + list(side_in_specs)
    args += list(extras) + list(side_in)
    kern = functools.partial(_nmm_kernel, splits=splits, has_bias=bias is not None,
                             n_extra=len(extras), n_out=n_out, epilogue=epilogue,
                             side_fn=side_fn, n_side_in=len(side_in),
                             n_side_out=len(side_out_shapes))
    outs = pl.pallas_call(
        kern,
        out_shape=[jax.ShapeDtypeStruct((m, ng), F32)] * n_out + list(side_out_shapes),
        grid=(m // tm, nj),
        in_specs=in_specs,
        out_specs=[pl.BlockSpec((tm, tn), lambda i, j: (i, j))] * n_out + list(side_out_specs),
        scratch_shapes=[pltpu.VMEM((tm, d), BF16)],
        compiler_params=_cparams(("parallel", "arbitrary")),
        name=name,
    )(*args)
    return outs


def _ep_plain(accs, e_refs, j):
    return [accs[0]]


def _ep_glu(accs, e_refs, j):
    return [accs[0] * _sigmoid(accs[1])]


def _ep_sconv(accs, e_refs, j):
    return [accs[0], accs[1] * accs[2]]


def _mlp_kernel(*refs, tf, final_norm):
    x_ref, g_ref, sh_ref, sc_ref, gate_ref, wu_ref, wd_ref = refs[:7]
    p = 7
    gf_ref = None
    if final_norm:
        gf_ref = refs[p]; p += 1
    o_ref = refs[p]
    x = x_ref[...]
    h = _normmod(x, g_ref[...], _rows2d(sh_ref), _rows2d(sc_ref)).astype(BF16)
    ff = wu_ref.shape[1]
    acc = jnp.zeros(x.shape, F32)
    for c in range(ff // tf):
        a = jnp.dot(h, wu_ref[:, c * tf:(c + 1) * tf], preferred_element_type=F32)
        a = jnp.square(jnp.maximum(a, 0.0)).astype(BF16)
        acc = acc + jnp.dot(a, wd_ref[c * tf:(c + 1) * tf, :], preferred_element_type=F32)
    y = x + _rows2d(gate_ref) * acc
    if final_norm:
        y = y * lax.rsqrt(jnp.mean(y * y, axis=-1, keepdims=True) + RMS_EPS) * gf_ref[...]
    o_ref[...] = y


def mlp_block(x, gnorm, sh, sc, gate, w_up, w_down, *, tm, rows_per_seq, tf=512,
              norm_final=None):
    m, d = x.shape
    in_specs = [pl.BlockSpec((tm, d), lambda i: (i, 0)),
                pl.BlockSpec((1, d), lambda i: (0, 0)),
                _mod_spec(sh, tm, rows_per_seq), _mod_spec(sc, tm, rows_per_seq),
                _mod_spec(gate, tm, rows_per_seq),
                _const_spec(w_up), _const_spec(w_down)]
    args = [x, gnorm.reshape(1, d), sh, sc, gate, w_up, w_down]
    if norm_final is not None:
        in_specs.append(pl.BlockSpec((1, d), lambda i: (0, 0)))
        args.append(norm_final.reshape(1, d))
    return pl.pallas_call(
        functools.partial(_mlp_kernel, tf=tf, final_norm=norm_final is not None),
        out_shape=jax.ShapeDtypeStruct((m, d), F32),
        grid=(m // tm,),
        in_specs=in_specs,
        out_specs=pl.BlockSpec((tm, d), lambda i: (i, 0)),
        compiler_params=_cparams(("parallel",)),
        name="mlp_block",
    )(*args)


def _pro_id(mains, pars):
    return mains[0]


def _pro_conf(mains, pars):
    bdw, lng, lnb = pars
    y = mains[0] + bdw
    mu = jnp.mean(y, axis=-1, keepdims=True)
    var = jnp.mean(jnp.square(y - mu), axis=-1, keepdims=True)
    yn = (y - mu) * lax.rsqrt(var + LN_EPS) * lng + lnb
    return _silu(yn)


def _pro_mul(mains, pars):
    return mains[1] * mains[0]


def _pro_gdn(mains, pars):
    o, z = mains
    ng = pars[0]
    outs = []
    for hh in range(GDN_HV):
        oh = o[:, hh * GDN_DV:(hh + 1) * GDN_DV]
        yh = oh * lax.rsqrt(jnp.mean(oh * oh, axis=-1, keepdims=True) + RMS_EPS) * ng
        outs.append(yh * _silu(z[:, hh * GDN_DV:(hh + 1) * GDN_DV]))
    return jnp.concatenate(outs, axis=-1)


def _tail_compute(a, w_ref, b_ref, x_ref, gate_ref, o_ref):
    out = jnp.dot(a.astype(BF16), w_ref[...], preferred_element_type=F32)
    if b_ref is not None:
        out = out + b_ref[...]
    o_ref[...] = x_ref[...] + _rows2d(gate_ref) * out


def _tail_kernel(*refs, n_main, n_par, has_bias, prologue):
    mains = [r[...] for r in refs[:n_main]]
    p = n_main
    pars = [r[...] for r in refs[p:p + n_par]]; p += n_par
    w_ref = refs[p]; p += 1
    b_ref = None
    if has_bias:
        b_ref = refs[p]; p += 1
    x_ref, gate_ref, o_ref = refs[p:p + 3]
    _tail_compute(prologue(mains, pars), w_ref, b_ref, x_ref, gate_ref, o_ref)


def _main_spec(a, tm):
    if isinstance(a, tuple):
        arr, width, cb = a
        return arr, pl.BlockSpec((tm, width), lambda i, cb=cb: (i, cb))
    return a, pl.BlockSpec((tm, a.shape[1]), lambda i: (i, 0))


def tail(mains, pars, w, bias, x, gate, *, prologue, tm, rows_per_seq, name):
    m, d = x.shape
    in_specs, args = [], []
    for a in mains:
        arr, spec = _main_spec(a, tm)
        in_specs.append(spec)
        args.append(arr)
    for prm in pars:
        prm = prm.reshape(1, -1)
        in_specs.append(_const_spec(prm))
        args.append(prm)
    in_specs.append(_const_spec(w)); args.append(w)
    if bias is not None:
        b2 = bias.reshape(1, -1)
        in_specs.append(_const_spec(b2)); args.append(b2)
    in_specs += [pl.BlockSpec((tm, d), lambda i: (i, 0)), _mod_spec(gate, tm, rows_per_seq)]
    args += [x, gate]
    return pl.pallas_call(
        functools.partial(_tail_kernel, n_main=len(mains), n_par=len(pars),
                          has_bias=bias is not None, prologue=prologue),
        out_shape=jax.ShapeDtypeStruct((m, d), F32),
        grid=(m // tm,),
        in_specs=in_specs,
        out_specs=pl.BlockSpec((tm, d), lambda i: (i, 0)),
        compiler_params=_cparams(("parallel",)),
        name=name,
    )(*args)


def _dwconv_tile(win_ref, w_ref, k_taps, base, r0, rows, l0, lc):
    acc = None
    for k in range(k_taps):
        t = w_ref[k:k + 1, l0:l0 + lc] * win_ref[r0 + base + k:r0 + base + k + rows, l0:l0 + lc]
        acc = t if acc is None else acc + t
    return acc


def _fill_window(win, cur_ref, halo_ref, halo, tiles_per_seq):
    first = (pl.program_id(0) % tiles_per_seq) == 0
    win[0:halo, :] = jnp.where(first, 0.0, halo_ref[...])
    win[halo:, :] = cur_ref[...]


def _conv_tail_kernel(*refs, k_taps, halo, tt, tiles_per_seq, rc, lc, n_main, n_par,
                      has_bias, prologue):
    u_ref, halo_ref, wc_ref = refs[:3]
    p = 3
    main_refs = refs[p:p + n_main]; p += n_main
    par_refs = refs[p:p + n_par]; p += n_par
    w_ref = refs[p]; p += 1
    b_ref = None
    if has_bias:
        b_ref = refs[p]; p += 1
    x_ref, gate_ref, o_ref, win, ybuf = refs[p:p + 5]
    _fill_window(win, u_ref, halo_ref, halo, tiles_per_seq)
    c = u_ref.shape[1]
    base = halo - (k_taps - 1)
    for r in range(tt // rc):
        for l in range(c // lc):
            ybuf[r * rc:(r + 1) * rc, l * lc:(l + 1) * lc] = _dwconv_tile(
                win, wc_ref, k_taps, base, r * rc, rc, l * lc, lc)
    mains = [ybuf[...]] + [r[...] for r in main_refs]
    pars = [r[...] for r in par_refs]
    _tail_compute(prologue(mains, pars), w_ref, b_ref, x_ref, gate_ref, o_ref)


def conv_tail(u, wconv, mains, pars, w, bias, x, gate, *, halo, tt, rows_per_seq,
              prologue, name, rc=32, lc=256):
    m, d = x.shape
    c = u.shape[1]
    k_taps = wconv.shape[0]
    hb = tt // halo
    in_specs = [pl.BlockSpec((tt, c), lambda i: (i, 0)),
                pl.BlockSpec((halo, c), lambda i: (jnp.maximum(i * hb - 1, 0), 0)),
                _const_spec(wconv)]
    args = [u, u, wconv]
    for a in mains:
        arr, spec = _main_spec(a, tt)
        in_specs.append(spec)
        args.append(arr)
    for prm in pars:
        prm = prm.reshape(1, -1)
        in_specs.append(_const_spec(prm)); args.append(prm)
    in_specs.append(_const_spec(w)); args.append(w)
    if bias is not None:
        b2 = bias.reshape(1, -1)
        in_specs.append(_const_spec(b2)); args.append(b2)
    in_specs += [pl.BlockSpec((tt, d), lambda i: (i, 0)), _mod_spec(gate, tt, rows_per_seq)]
    args += [x, gate]
    kern = functools.partial(
        _conv_tail_kernel, k_taps=k_taps, halo=halo, tt=tt, tiles_per_seq=rows_per_seq // tt,
        rc=rc, lc=lc, n_main=len(mains), n_par=len(pars), has_bias=bias is not None,
        prologue=prologue)
    return pl.pallas_call(
        kern,
        out_shape=jax.ShapeDtypeStruct((m, d), F32),
        grid=(m // tt,),
        in_specs=in_specs,
        out_specs=pl.BlockSpec((tt, d), lambda i: (i, 0)),
        scratch_shapes=[pltpu.VMEM((halo + tt, c), F32), pltpu.VMEM((tt, c), F32)],
        compiler_params=_cparams(("parallel",)),
        name=name,
    )(*args)


def _conv_step_kernel(st_ref, u_ref, w_ref, y_ref, ns_ref, *, k_taps, c, lc, act):
    for l in range(c // lc):
        u = u_ref[:, l * lc:(l + 1) * lc]
        acc = w_ref[k_taps - 1:k_taps, l * lc:(l + 1) * lc] * u
        for k in range(k_taps - 1):
            s = st_ref[:, k * c + l * lc:k * c + (l + 1) * lc]
            acc = acc + w_ref[k:k + 1, l * lc:(l + 1) * lc] * s
            if k >= 1:
                ns_ref[:, (k - 1) * c + l * lc:(k - 1) * c + (l + 1) * lc] = s
        ns_ref[:, (k_taps - 2) * c + l * lc:(k_taps - 2) * c + (l + 1) * lc] = u
        y_ref[:, l * lc:(l + 1) * lc] = act(acc)


def conv_step(state, u, wconv, *, act=lambda v: v, bb=32, name):
    bn, km1, c = state.shape
    k_taps = km1 + 1
    bb = min(bb, bn)
    st2 = state.reshape(bn, km1 * c)
    y, ns = pl.pallas_call(
        functools.partial(_conv_step_kernel, k_taps=k_taps, c=c, lc=min(c, 1024), act=act),
        out_shape=[jax.ShapeDtypeStruct((bn, c), F32),
                   jax.ShapeDtypeStruct((bn, km1 * c), F32)],
        grid=(bn // bb,),
        in_specs=[pl.BlockSpec((bb, km1 * c), lambda i: (i, 0)),
                  pl.BlockSpec((bb, c), lambda i: (i, 0)),
                  _const_spec(wconv)],
        out_specs=[pl.BlockSpec((bb, c), lambda i: (i, 0)),
                   pl.BlockSpec((bb, km1 * c), lambda i: (i, 0))],
        compiler_params=_cparams(("parallel",)),
        name=name,
    )(st2, u, wconv)
    return y, ns.reshape(bn, km1, c)


QKV_TN = 256
N_ROPE_TILES = (N_HEADS + N_KV) * HEAD_DIM // QKV_TN


def rope_tables(pos):
    half = ROT_DIM // 2
    inv = jnp.power(jnp.float32(ROPE_THETA), -jnp.arange(half, dtype=F32) * 2.0 / ROT_DIM)
    ang = pos.astype(F32)[:, None] * inv[None, :]
    cos, sin = jnp.cos(ang), jnp.sin(ang)
    lane = jnp.arange(LANES) % HEAD_DIM
    idx = lane % half
    cosf = jnp.where(lane < ROT_DIM, cos[:, idx], 1.0)
    sina = jnp.where(lane < half, -sin[:, idx], 0.0)
    sinb = jnp.where((lane >= half) & (lane < ROT_DIM), sin[:, idx], 0.0)
    return cosf, sina, sinb


def _ep_rope(accs, e_refs, j):
    a = accs[0]
    cosf, sina, sinb = [r[...] for r in e_refs]
    half = ROT_DIM // 2
    outs = []
    for g in range(a.shape[1] // LANES):
        xg = a[:, g * LANES:(g + 1) * LANES]
        outs.append(xg * cosf + pltpu.roll(xg, LANES - half, 1) * sina
                    + pltpu.roll(xg, half, 1) * sinb)
    roped = jnp.concatenate(outs, axis=-1)
    return [jnp.where(j < N_ROPE_TILES, roped, a)]


def _dot_nt(a, b):
    return lax.dot_general(a, b, (((1,), (1,)), ((), ())), preferred_element_type=F32)


def _dot_tn(a, b):
    return lax.dot_general(a, b, (((0,), (0,)), ((), ())), preferred_element_type=F32)


def _swa_kernel(sink_ref, q_ref, kc_ref, kp_ref, vc_ref, vp_ref, o_ref):
    n = pl.program_id(1)
    w = WINDOW
    lane = lax.broadcasted_iota(jnp.int32, (1, LANES), 1)
    lo = lane < HEAD_DIM
    row = lax.broadcasted_iota(jnp.int32, (2 * w, 1), 0)
    qi = row % w
    kj = lax.broadcasted_iota(jnp.int32, (1, 2 * w), 1)
    band = (kj >= qi) & (kj <= qi + w) & ((kj >= w) | (n > 0))
    top = row < w
    scale = HEAD_DIM ** -0.5
    for h in range(N_KV):
        cg = (h // 2) * LANES
        kg = jnp.concatenate([kp_ref[:, cg:cg + LANES], kc_ref[:, cg:cg + LANES]], axis=0)
        vg = jnp.concatenate([vp_ref[:, cg:cg + LANES], vc_ref[:, cg:cg + LANES]], axis=0)
        kr = pltpu.roll(kg, HEAD_DIM, 1)
        vr = pltpu.roll(vg, HEAD_DIM, 1)
        if h % 2 == 0:
            k_lo, k_hi = jnp.where(lo, kg, 0.0), jnp.where(lo, 0.0, kr)
            v_lo, v_hi = jnp.where(lo, vg, 0.0), jnp.where(lo, 0.0, vr)
        else:
            k_lo, k_hi = jnp.where(lo, kr, 0.0), jnp.where(lo, 0.0, kg)
            v_lo, v_hi = jnp.where(lo, vr, 0.0), jnp.where(lo, 0.0, vg)
        c0 = h * 4 * HEAD_DIM
        q2 = jnp.concatenate([q_ref[:, c0:c0 + LANES], q_ref[:, c0 + LANES:c0 + 2 * LANES]],
                             axis=0).astype(BF16)
        o2 = None
        for par, kpad, vpad in ((0, k_lo, v_lo), (1, k_hi, v_hi)):
            s = _dot_nt(q2, kpad.astype(BF16)) * scale
            s = jnp.where(band, s, -jnp.inf)
            sink = jnp.where(top, sink_ref[4 * h + par], sink_ref[4 * h + 2 + par])
            m = jnp.maximum(jnp.max(s, axis=-1, keepdims=True), sink)
            p = jnp.exp(s - m)
            den = jnp.sum(p, axis=-1, keepdims=True) + jnp.exp(sink - m)
            part = jnp.dot(p.astype(BF16), vpad.astype(BF16), preferred_element_type=F32) / den
            o2 = part if o2 is None else o2 + part
        o_ref[:, c0:c0 + LANES] = o2[:w]
        o_ref[:, c0 + LANES:c0 + 2 * LANES] = o2[w:]


def swa_prompt(qkv, sinks, *, bsz, seq):
    nb = seq // WINDOW
    dq = N_HEADS * HEAD_DIM
    dkv = N_KV * HEAD_DIM
    kcol, vcol = dq // dkv, dq // dkv + 1
    cur = lambda b, n: b * nb + n
    prev = lambda b, n: b * nb + jnp.maximum(n - 1, 0)
    return pl.pallas_call(
        _swa_kernel,
        out_shape=jax.ShapeDtypeStruct((bsz * seq, dq), F32),
        grid=(bsz, nb),
        in_specs=[pl.BlockSpec(memory_space=pltpu.SMEM),
                  pl.BlockSpec((WINDOW, dq), lambda b, n: (cur(b, n), 0)),
                  pl.BlockSpec((WINDOW, dkv), lambda b, n: (cur(b, n), kcol)),
                  pl.BlockSpec((WINDOW, dkv), lambda b, n: (prev(b, n), kcol)),
                  pl.BlockSpec((WINDOW, dkv), lambda b, n: (cur(b, n), vcol)),
                  pl.BlockSpec((WINDOW, dkv), lambda b, n: (prev(b, n), vcol))],
        out_specs=pl.BlockSpec((WINDOW, dq), lambda b, n: (cur(b, n), 0)),
        compiler_params=_cparams(("parallel", "parallel")),
        name="swa_prompt",
    )(sinks, qkv, qkv, qkv, qkv, qkv)


def _swa_decode_kernel(sink_ref, q_ref, kn_ref, vn_ref, kc_ref, vc_ref, o_ref, ko_ref, vo_ref,
                       *, bb):
    w = WINDOW
    dkv = N_KV * HEAD_DIM
    lane_head = lax.broadcasted_iota(jnp.int32, (N_KV, dkv), 1) // HEAD_DIM
    row_head = lax.broadcasted_iota(jnp.int32, (N_KV, dkv), 0)
    own = lane_head == row_head
    sinks = sink_ref[...]
    scale = HEAD_DIM ** -0.5

    def body(b, carry):
        qb = q_ref[b]
        qblk = jnp.concatenate(
            [jnp.where(own, jnp.broadcast_to(qb[g:g + 1, :], (N_KV, dkv)), 0.0)
             for g in range(N_HEADS // N_KV)], axis=0)
        kc = kc_ref[b]
        vc = vc_ref[b]
        kn = kn_ref[b]
        vn = vn_ref[b]
        qb16 = qblk.astype(BF16)
        s_c = _dot_nt(qb16, kc.astype(BF16)) * scale
        s_n = jnp.sum(qblk * kn, axis=-1, keepdims=True) * scale
        m = jnp.maximum(jnp.maximum(jnp.max(s_c, axis=-1, keepdims=True), s_n), sinks)
        p_c = jnp.exp(s_c - m)
        p_n = jnp.exp(s_n - m)
        den = jnp.sum(p_c, axis=-1, keepdims=True) + p_n + jnp.exp(sinks - m)
        p_c = p_c / den
        p_n = p_n / den
        o = (jnp.dot(p_c.astype(BF16), vc.astype(BF16), preferred_element_type=F32)
             + p_n * vn)
        for g in range(N_HEADS // N_KV):
            og = jnp.sum(jnp.where(own, o[g * N_KV:(g + 1) * N_KV, :], 0.0), axis=0,
                         keepdims=True)
            o_ref[b, g:g + 1, :] = og
        ko_ref[b, 0:w - 1, :] = kc[1:w, :]
        ko_ref[b, w - 1:w, :] = kn
        vo_ref[b, 0:w - 1, :] = vc[1:w, :]
        vo_ref[b, w - 1:w, :] = vn
        return carry

    lax.fori_loop(0, bb, body, 0)


def swa_decode(qg, k_new, v_new, k_cache, v_cache, sinks_gh, *, bb=8):
    bn = qg.shape[0]
    dkv = N_KV * HEAD_DIM
    ng = N_HEADS // N_KV
    return pl.pallas_call(
        functools.partial(_swa_decode_kernel, bb=bb),
        out_shape=[jax.ShapeDtypeStruct((bn, ng, dkv), F32),
                   jax.ShapeDtypeStruct((bn, WINDOW, dkv), F32),
                   jax.ShapeDtypeStruct((bn, WINDOW, dkv), F32)],
        grid=(bn // bb,),
        in_specs=[_const_spec(sinks_gh),
                  pl.BlockSpec((bb, ng, dkv), lambda i: (i, 0, 0)),
                  pl.BlockSpec((bb, 1, dkv), lambda i: (i, 0, 0)),
                  pl.BlockSpec((bb, 1, dkv), lambda i: (i, 0, 0)),
                  pl.BlockSpec((bb, WINDOW, dkv), lambda i: (i, 0, 0)),
                  pl.BlockSpec((bb, WINDOW, dkv), lambda i: (i, 0, 0))],
        out_specs=[pl.BlockSpec((bb, ng, dkv), lambda i: (i, 0, 0)),
                   pl.BlockSpec((bb, WINDOW, dkv), lambda i: (i, 0, 0)),
                   pl.BlockSpec((bb, WINDOW, dkv), lambda i: (i, 0, 0))],
        compiler_params=_cparams(("parallel",)),
        name="swa_decode",
    )(sinks_gh, qg, k_new.reshape(bn, 1, dkv), v_new.reshape(bn, 1, dkv), k_cache, v_cache)


GDN_BA_PAD = LANES


def _gdn_side(h, si_refs, so_refs):
    wba_ref, wbat_ref = si_refs
    ba_ref, bat_ref = so_refs
    ba_ref[...] = jnp.dot(h, wba_ref[...], preferred_element_type=F32)
    bat_ref[...] = _dot_nt(wbat_ref[...], h)


def _gdn_gates(b, a, alog, dtb):
    x = a + dtb
    sp = jnp.maximum(x, 0.0) + jnp.log1p(jnp.exp(-jnp.abs(x)))
    return -jnp.exp(alog) * sp, _sigmoid(b)


def _l2n(x):
    return x * lax.rsqrt(jnp.sum(x * x, axis=-1, keepdims=True) + L2_EPS)


def _gdn_store_qkv(y, col, rows, q_ref, k_ref, v_ref):
    if col < GDN_QK_W:
        q_ref[rows, col:col + GDN_DK] = _l2n(y) * (GDN_DK ** -0.5)
    elif col < 2 * GDN_QK_W:
        k_ref[rows, col - GDN_QK_W:col - GDN_QK_W + GDN_DK] = _l2n(y)
    else:
        v_ref[rows, col - 2 * GDN_QK_W:col - 2 * GDN_QK_W + GDN_DV] = y


def _gdn_prep_kernel(cur_ref, halo_ref, wc_ref, ba_ref, bat_ref, alr_ref, dtr_ref, alc_ref,
                     dtc_ref, q_ref, k_ref, v_ref, g_ref, b_ref, gt_ref, win, *, tt,
                     tiles_per_seq, rc):
    halo = SUBLANES
    _fill_window(win, cur_ref, halo_ref, halo, tiles_per_seq)
    base = halo - (GDN_CONV_K - 1)
    for r in range(tt // rc):
        rows = slice(r * rc, (r + 1) * rc)
        for l in range(GDN_CONV_DIM // LANES):
            y = _silu(_dwconv_tile(win, wc_ref, GDN_CONV_K, base, r * rc, rc, l * LANES, LANES))
            _gdn_store_qkv(y, l * LANES, rows, q_ref, k_ref, v_ref)
    ba = ba_ref[...]
    g, beta = _gdn_gates(ba[:, 0:GDN_HV], ba[:, GDN_HV:2 * GDN_HV], alr_ref[...], dtr_ref[...])
    g_ref[...] = g
    b_ref[...] = beta
    bat = bat_ref[...]
    gt, _ = _gdn_gates(bat[0:GDN_HV, :], bat[GDN_HV:2 * GDN_HV, :], alc_ref[...], dtc_ref[...])
    for ci in range(tt // GDN_CHUNK):
        gt_ref[ci] = gt[:, ci * GDN_CHUNK:(ci + 1) * GDN_CHUNK]


def gdn_prep_prompt(proj, ba, bat, wconv, a_log, dt_bias, *, tt, rows_per_seq, rc=128):
    m = proj.shape[0]
    c = GDN_CONV_DIM
    hb = tt // SUBLANES
    alr, dtr = a_log.reshape(1, GDN_HV), dt_bias.reshape(1, GDN_HV)
    alc, dtc = a_log.reshape(GDN_HV, 1), dt_bias.reshape(GDN_HV, 1)
    return pl.pallas_call(
        functools.partial(_gdn_prep_kernel, tt=tt, tiles_per_seq=rows_per_seq // tt, rc=rc),
        out_shape=[jax.ShapeDtypeStruct((m, GDN_QK_W), F32),
                   jax.ShapeDtypeStruct((m, GDN_QK_W), F32),
                   jax.ShapeDtypeStruct((m, GDN_V_W), F32),
                   jax.ShapeDtypeStruct((m, GDN_HV), F32),
                   jax.ShapeDtypeStruct((m, GDN_HV), F32),
                   jax.ShapeDtypeStruct((m // GDN_CHUNK, GDN_HV, GDN_CHUNK), F32)],
        grid=(m // tt,),
        in_specs=[pl.BlockSpec((tt, c), lambda i: (i, 0)),
                  pl.BlockSpec((SUBLANES, c), lambda i: (jnp.maximum(i * hb - 1, 0), 0)),
                  _const_spec(wconv),
                  pl.BlockSpec((tt, GDN_BA_PAD), lambda i: (i, 0)),
                  pl.BlockSpec((2 * GDN_HV, tt), lambda i: (0, i)),
                  _const_spec(alr), _const_spec(dtr), _const_spec(alc), _const_spec(dtc)],
        out_specs=[pl.BlockSpec((tt, GDN_QK_W), lambda i: (i, 0)),
                   pl.BlockSpec((tt, GDN_QK_W), lambda i: (i, 0)),
                   pl.BlockSpec((tt, GDN_V_W), lambda i: (i, 0)),
                   pl.BlockSpec((tt, GDN_HV), lambda i: (i, 0)),
                   pl.BlockSpec((tt, GDN_HV), lambda i: (i, 0)),
                   pl.BlockSpec((tt // GDN_CHUNK, GDN_HV, GDN_CHUNK), lambda i: (i, 0, 0))],
        scratch_shapes=[pltpu.VMEM((SUBLANES + tt, c), F32)],
        compiler_params=_cparams(("parallel",)),
        name="gdn_prep_prompt",
    )(proj, proj, wconv, ba, bat, alr, dtr, alc, dtc)


def _gdn_prep_step_kernel(y_ref, ba_ref, alr_ref, dtr_ref, q_ref, k_ref, v_ref, g_ref, b_ref):
    rows = slice(None)
    for l in range(GDN_CONV_DIM // LANES):
        _gdn_store_qkv(y_ref[:, l * LANES:(l + 1) * LANES], l * LANES, rows, q_ref, k_ref, v_ref)
    ba = ba_ref[...]
    g, beta = _gdn_gates(ba[:, 0:GDN_HV], ba[:, GDN_HV:2 * GDN_HV], alr_ref[...], dtr_ref[...])
    g_ref[...] = g
    b_ref[...] = beta


def gdn_prep_step(y, ba, a_log, dt_bias):
    bn = y.shape[0]
    return pl.pallas_call(
        _gdn_prep_step_kernel,
        out_shape=[jax.ShapeDtypeStruct((bn, GDN_QK_W), F32),
                   jax.ShapeDtypeStruct((bn, GDN_QK_W), F32),
                   jax.ShapeDtypeStruct((bn, GDN_V_W), F32),
                   jax.ShapeDtypeStruct((bn, GDN_HV), F32),
                   jax.ShapeDtypeStruct((bn, GDN_HV), F32)],
        compiler_params=pltpu.CompilerParams(vmem_limit_bytes=VMEM_LIMIT),
        name="gdn_prep_step",
    )(y, ba, a_log.reshape(1, GDN_HV), dt_bias.reshape(1, GDN_HV))


def _dot_hi(a, b):
    return jnp.dot(a, b, precision=lax.Precision.HIGHEST, preferred_element_type=F32)


def _unit_lower_inverse(l_strict, eye, n):
    p = eye - l_strict
    lp = l_strict
    for _ in range(int(math.log2(n)) - 1):
        lp = _dot_hi(lp, lp)
        p = p + _dot_hi(p, lp)
    return p


def _gdn_chunk_kernel(q_ref, k_ref, v_ref, g_ref, b_ref, gt_ref, o_ref, sfin_ref, s_scr, *, c):
    n = pl.program_id(1)

    @pl.when(n == 0)
    def _():
        s_scr[...] = jnp.zeros(s_scr.shape, F32)

    row = lax.broadcasted_iota(jnp.int32, (c, c), 0)
    col = lax.broadcasted_iota(jnp.int32, (c, c), 1)
    lower = row >= col
    strict = row > col
    eye = (row == col).astype(F32)
    g = g_ref[...]
    beta = b_ref[...]
    gc = _dot_hi(lower.astype(F32), g)
    gct = _dot_hi(gt_ref[0], (row <= col).astype(F32))
    eg = jnp.exp(gc)
    glast = gc[c - 1:c, :]
    ek = jnp.exp(glast - gc)
    egl = jnp.exp(glast)
    rep = GDN_HV // GDN_HK
    for hk in range(GDN_HK):
        kh = k_ref[:, hk * GDN_DK:(hk + 1) * GDN_DK]
        qh = q_ref[:, hk * GDN_DK:(hk + 1) * GDN_DK]
        kh16 = kh.astype(BF16)
        a_all = _dot_nt(jnp.concatenate([kh16, qh.astype(BF16)], axis=0), kh16)
        a_kk, a_qk = a_all[:c], a_all[c:]
        for hv in range(hk * rep, (hk + 1) * rep):
            diff = gc[:, hv:hv + 1] - gct[hv:hv + 1, :]
            decay = jnp.exp(jnp.where(lower, diff, -jnp.inf))
            bcol = beta[:, hv:hv + 1]
            l_strict = jnp.where(strict, bcol * a_kk * decay, 0.0)
            t_inv = _unit_lower_inverse(l_strict, eye, c)
            vh = v_ref[:, hv * GDN_DV:(hv + 1) * GDN_DV]
            egc = eg[:, hv:hv + 1]
            rhs = jnp.concatenate([vh * bcol, kh * bcol * egc], axis=1)
            uw = jnp.dot(t_inv.astype(BF16), rhs.astype(BF16), preferred_element_type=F32)
            s_old = s_scr[hv]
            wq = jnp.concatenate([uw[:, GDN_DV:], qh * egc], axis=0)
            wqs = jnp.dot(wq.astype(BF16), s_old.astype(BF16), preferred_element_type=F32)
            v_new = uw[:, :GDN_DV] - wqs[:c]
            v16 = v_new.astype(BF16)
            attn = (a_qk * decay).astype(BF16)
            o_ref[:, hv * GDN_DV:(hv + 1) * GDN_DV] = wqs[c:] + jnp.dot(
                attn, v16, preferred_element_type=F32)
            kd = (kh * ek[:, hv:hv + 1]).astype(BF16)
            s_scr[hv] = s_old * egl[:, hv:hv + 1] + _dot_tn(kd, v16)

    @pl.when(n == pl.num_programs(1) - 1)
    def _():
        sfin_ref[0] = s_scr[...]


def gdn_chunked(q, k, v, g, beta, gt, *, bsz, seq, c=GDN_CHUNK):
    nc = seq // c
    idx = lambda b, n: (b * nc + n, 0)
    return pl.pallas_call(
        functools.partial(_gdn_chunk_kernel, c=c),
        out_shape=[jax.ShapeDtypeStruct((bsz * seq, GDN_V_W), F32),
                   jax.ShapeDtypeStruct((bsz, GDN_HV, GDN_DK, GDN_DV), F32)],
        grid=(bsz, nc),
        in_specs=[pl.BlockSpec((c, GDN_QK_W), idx), pl.BlockSpec((c, GDN_QK_W), idx),
                  pl.BlockSpec((c, GDN_V_W), idx), pl.BlockSpec((c, GDN_HV), idx),
                  pl.BlockSpec((c, GDN_HV), idx),
                  pl.BlockSpec((1, GDN_HV, c), lambda b, n: (b * nc + n, 0, 0))],
        out_specs=[pl.BlockSpec((c, GDN_V_W), idx),
                   pl.BlockSpec((1, GDN_HV, GDN_DK, GDN_DV), lambda b, n: (b, 0, 0, 0))],
        scratch_shapes=[pltpu.VMEM((GDN_HV, GDN_DK, GDN_DV), F32)],
        compiler_params=_cparams(("parallel", "arbitrary")),
        name="gdn_chunked",
    )(q, k, v, g, beta, gt)


def _gdn_decode_kernel(q_ref, k_ref, v_ref, g_ref, b_ref, s_ref, o_ref, so_ref, qt_scr, kt_scr):
    b = pl.program_id(0)

    @pl.when(b == 0)
    def _():
        for hk in range(GDN_HK):
            qt_scr[hk] = q_ref[:, hk * GDN_DK:(hk + 1) * GDN_DK].T
            kt_scr[hk] = k_ref[:, hk * GDN_DK:(hk + 1) * GDN_DK].T

    pick = lax.broadcasted_iota(jnp.int32, (1, q_ref.shape[0]), 1) == b
    grow = g_ref[0]
    brow = b_ref[0]
    rep = GDN_HV // GDN_HK
    for hv in range(GDN_HV):
        hk = hv // rep
        kcol = jnp.sum(jnp.where(pick, kt_scr[hk], 0.0), axis=1, keepdims=True)
        qcol = jnp.sum(jnp.where(pick, qt_scr[hk], 0.0), axis=1, keepdims=True)
        s1 = s_ref[0, hv] * jnp.exp(grow[:, hv:hv + 1])
        kv = jnp.sum(kcol * s1, axis=0, keepdims=True)
        vrow = v_ref[0, :, hv * GDN_DV:(hv + 1) * GDN_DV]
        s2 = s1 + kcol * ((vrow - kv) * brow[:, hv:hv + 1])
        so_ref[0, hv] = s2
        o_ref[0, :, hv * GDN_DV:(hv + 1) * GDN_DV] = jnp.sum(qcol * s2, axis=0, keepdims=True)


def gdn_decode(q, k, v, g, beta, state):
    bn = q.shape[0]
    sspec = pl.BlockSpec((1, GDN_HV, GDN_DK, GDN_DV), lambda b: (b, 0, 0, 0))
    rowspec = lambda width: pl.BlockSpec((1, 1, width), lambda b: (b, 0, 0))
    o, s_new = pl.pallas_call(
        _gdn_decode_kernel,
        out_shape=[jax.ShapeDtypeStruct((bn, 1, GDN_V_W), F32),
                   jax.ShapeDtypeStruct(state.shape, F32)],
        grid=(bn,),
        in_specs=[_const_spec(q), _const_spec(k), rowspec(GDN_V_W), rowspec(GDN_HV),
                  rowspec(GDN_HV), sspec],
        out_specs=[rowspec(GDN_V_W), sspec],
        scratch_shapes=[pltpu.VMEM((GDN_HK, GDN_DK, bn), F32),
                        pltpu.VMEM((GDN_HK, GDN_DK, bn), F32)],
        compiler_params=_cparams(("arbitrary",)),
        name="gdn_decode",
    )(q, k, v.reshape(bn, 1, GDN_V_W), g.reshape(bn, 1, GDN_HV), beta.reshape(bn, 1, GDN_HV),
      state)
    return o.reshape(bn, GDN_V_W), s_new


N_MIXERS = 4
TM_PROMPT = 512
TT_PROMPT = 256
ADA_ROWS_PAD = SUBLANES


def _layer_mods(mods, i, lo, hi, per_seq):
    d = D_MODEL
    out = []
    for k in range(6):
        v = mods[i, lo:hi, k * d:(k + 1) * d]
        out.append(v.reshape(hi - lo, 1, d) if per_seq else v)
    return out


def _gdn_weights(w_in):
    n_main = GDN_CONV_DIM + GDN_V_W
    w_ba = w_in[:, n_main:n_main + 2 * GDN_HV]
    w_ba_pad = jnp.pad(w_ba, ((0, 0), (0, GDN_BA_PAD - 2 * GDN_HV)))
    return w_in[:, :n_main].astype(BF16), w_ba_pad.astype(BF16), w_ba.T.astype(BF16)


def _gdn_inproj(x, gnorm, sh, sc, w_main, w_ba, w_bat, *, tm, rows_per_seq):
    m = x.shape[0]
    side = (_gdn_side, (w_ba, w_bat), (_const_spec(w_ba), _const_spec(w_bat)),
            (jax.ShapeDtypeStruct((m, GDN_BA_PAD), F32),
             jax.ShapeDtypeStruct((2 * GDN_HV, m), F32)),
            (pl.BlockSpec((tm, GDN_BA_PAD), lambda i, j: (i, 0)),
             pl.BlockSpec((2 * GDN_HV, tm), lambda i, j: (0, i))))
    return nmm(x, gnorm, sh, sc, w_main, None, splits=1, n_out=1, epilogue=_ep_plain, tm=tm,
               tn=1024, rows_per_seq=rows_per_seq, side=side, name="gdn_inproj")


def _trunk(x, mods_of, prompt, bsz, seq, st, P):
    d = D_MODEL
    tm = TM_PROMPT if prompt else x.shape[0]
    rps = seq if prompt else 1
    new = {}
    for i in range(DEPTH):
        mixer, j = i % N_MIXERS, i // N_MIXERS
        sh1, sc1, g1, sh2, sc2, g2 = mods_of(i)
        gn = P['norm_mix'][i]
        if mixer == 0:
            u, = nmm(x, gn, sh1, sc1, P['conf_w_pw1'][j].astype(BF16), P['conf_b_pw1'][j],
                     splits=2, n_out=1, epilogue=_ep_glu, tm=tm, tn=512, rows_per_seq=rps,
                     name="conf_pw1")
            pars = [P['conf_b_dw'][j], P['conf_ln_g'][j], P['conf_ln_b'][j]]
            w2, b2 = P['conf_w_pw2'][j].astype(BF16), P['conf_b_pw2'][j]
            if prompt:
                x = conv_tail(u, P['conf_w_dw'][j], [], pars, w2, b2, x, g1, halo=32,
                              tt=TT_PROMPT, rows_per_seq=rps, prologue=_pro_conf,
                              name="conf_conv")
                new['conf'] = u.reshape(bsz, seq, d)[:, seq - (CONF_K - 1):]
            else:
                y, new['conf'] = conv_step(st['conf'][j], u, P['conf_w_dw'][j], name="conf_step")
                x = tail([y], pars, w2, b2, x, g1, prologue=_pro_conf, tm=tm, rows_per_seq=rps,
                         name="conf_tail")
        elif mixer == 1:
            if prompt:
                tabs = rope_tables(jnp.arange(seq, dtype=jnp.int32))
                tspec = pl.BlockSpec((tm, LANES), lambda r, c: (r % (seq // tm), 0))
            else:
                pos = jnp.full((tm,), PAST_LEN, dtype=jnp.int32)
                tabs = rope_tables(pos)
                tspec = pl.BlockSpec((tm, LANES), lambda r, c: (0, 0))
            qkv, = nmm(x, gn, sh1, sc1, P['swa_w_qkv'][j].astype(BF16), None, splits=1,
                       n_out=1, epilogue=_ep_rope, tm=tm, tn=QKV_TN, rows_per_seq=rps,
                       extras=tabs, extra_specs=[tspec] * 3, name="swa_qkv")
            dq, dkv = N_HEADS * HEAD_DIM, N_KV * HEAD_DIM
            sinks = P['swa_sinks'][j]
            if prompt:
                o = swa_prompt(qkv, sinks, bsz=bsz, seq=seq)
                last = qkv.reshape(bsz, seq, dq + 2 * dkv)[:, seq - WINDOW:]
                new['k'] = last[..., dq:dq + dkv].reshape(bsz, WINDOW, N_KV, HEAD_DIM)
                new['v'] = last[..., dq + dkv:].reshape(bsz, WINDOW, N_KV, HEAD_DIM)
            else:
                bn = x.shape[0]
                ng = N_HEADS // N_KV
                qg = qkv[:, :dq].reshape(bn, N_KV, ng, HEAD_DIM).transpose(0, 2, 1, 3)
                sinks_gh = sinks.reshape(N_KV, ng).T.reshape(N_HEADS, 1)
                og, ko, vo = swa_decode(qg.reshape(bn, ng, dkv), qkv[:, dq:dq + dkv],
                                        qkv[:, dq + dkv:], st['k'][j].reshape(bn, WINDOW, dkv),
                                        st['v'][j].reshape(bn, WINDOW, dkv), sinks_gh)
                o = og.reshape(bn, ng, N_KV, HEAD_DIM).transpose(0, 2, 1, 3).reshape(bn, dq)
                new['k'] = ko.reshape(bn, WINDOW, N_KV, HEAD_DIM)
                new['v'] = vo.reshape(bn, WINDOW, N_KV, HEAD_DIM)
            x = tail([o], [], P['swa_w_o'][j].astype(BF16), None, x, g1, prologue=_pro_id,
                     tm=tm, rows_per_seq=rps, name="swa_out")
        elif mixer == 2:
            w_main, w_ba, w_bat = _gdn_weights(P['gdn_w_in'][j])
            proj, ba, bat = _gdn_inproj(x, gn, sh1, sc1, w_main, w_ba, w_bat, tm=tm,
                                        rows_per_seq=rps)
            wc, al, dtb = P['gdn_w_conv'][j], P['gdn_a_log'][j], P['gdn_dt_bias'][j]
            if prompt:
                q, k, v, g, beta, gt = gdn_prep_prompt(proj, ba, bat, wc, al, dtb, tt=TT_PROMPT,
                                                       rows_per_seq=rps)
                o, new['ssm'] = gdn_chunked(q, k, v, g, beta, gt, bsz=bsz, seq=seq)
                new['gconv'] = proj.reshape(bsz, seq, -1)[:, seq - (GDN_CONV_K - 1):,
                                                          :GDN_CONV_DIM]
            else:
                y, new['gconv'] = conv_step(st['gconv'][j], proj[:, :GDN_CONV_DIM], wc,
                                            act=_silu, name="gdn_conv_step")
                q, k, v, g, beta = gdn_prep_step(y, ba, al, dtb)
                o, new['ssm'] = gdn_decode(q, k, v, g, beta, st['ssm'][j])
            x = tail([o, (proj, GDN_V_W, GDN_CONV_DIM // GDN_V_W)], [P['gdn_norm'][j]],
                     P['gdn_w_o'][j].astype(BF16), None, x, g1, prologue=_pro_gdn, tm=tm,
                     rows_per_seq=rps, name="gdn_out")
        else:
            gb, p = nmm(x, gn, sh1, sc1, P['sconv_w_in'][j].astype(BF16), None, splits=3,
                        n_out=2, epilogue=_ep_sconv, tm=tm, tn=512, rows_per_seq=rps,
                        name="sconv_in")
            wo = P['sconv_w_out'][j].astype(BF16)
            if prompt:
                x = conv_tail(p, P['sconv_w_conv'][j], [gb], [], wo, None, x, g1, halo=SUBLANES,
                              tt=TT_PROMPT, rows_per_seq=rps, prologue=_pro_mul,
                              name="sconv_conv")
                new['sconv'] = p.reshape(bsz, seq, d)[:, seq - (SCONV_K - 1):]
            else:
                y, new['sconv'] = conv_step(st['sconv'][j], p, P['sconv_w_conv'][j],
                                            name="sconv_step")
                x = tail([y, gb], [], wo, None, x, g1, prologue=_pro_mul, tm=tm,
                         rows_per_seq=rps, name="sconv_tail")
        x = mlp_block(x, P['norm_mlp'][i], sh2, sc2, g2, P['w_up'][i].astype(BF16),
                      P['w_down'][i].astype(BF16), tm=tm, rows_per_seq=rps,
                      norm_final=P['norm_final'] if i == DEPTH - 1 else None)
    return x, new


def kernel(x_prompt, x_sample, c_prompt, c_sample, state_conf_conv, cache_swa_k, cache_swa_v,
           state_gdn_ssm, state_gdn_conv, state_sconv, w_ada, b_ada, norm_mix, norm_mlp,
           w_up, w_down, norm_final, conf_w_pw1, conf_b_pw1, conf_w_dw, conf_b_dw, conf_ln_g,
           conf_ln_b, conf_w_pw2, conf_b_pw2, swa_w_qkv, swa_w_o, swa_sinks, gdn_w_in,
           gdn_w_conv, gdn_a_log, gdn_dt_bias, gdn_norm, gdn_w_o, sconv_w_in, sconv_w_conv,
           sconv_w_out):
    P = dict(norm_mix=norm_mix, norm_mlp=norm_mlp, w_up=w_up, w_down=w_down,
             norm_final=norm_final, conf_w_pw1=conf_w_pw1, conf_b_pw1=conf_b_pw1,
             conf_w_dw=conf_w_dw, conf_b_dw=conf_b_dw, conf_ln_g=conf_ln_g, conf_ln_b=conf_ln_b,
             conf_w_pw2=conf_w_pw2, conf_b_pw2=conf_b_pw2, swa_w_qkv=swa_w_qkv, swa_w_o=swa_w_o,
             swa_sinks=swa_sinks, gdn_w_in=gdn_w_in, gdn_w_conv=gdn_w_conv,
             gdn_a_log=gdn_a_log, gdn_dt_bias=gdn_dt_bias, gdn_norm=gdn_norm, gdn_w_o=gdn_w_o,
             sconv_w_in=sconv_w_in, sconv_w_conv=sconv_w_conv, sconv_w_out=sconv_w_out)
    bsz, seq, d = x_prompt.shape
    bn = x_sample.shape[0]
    n_c = bsz + bn
    pad = (-n_c) % ADA_ROWS_PAD
    c_all = jnp.concatenate([c_prompt, c_sample, jnp.zeros((pad, d), F32)], axis=0)
    mods = ada_all(c_all, w_ada, b_ada)
    st = dict(conf=state_conf_conv, k=cache_swa_k, v=cache_swa_v, ssm=state_gdn_ssm,
              gconv=state_gdn_conv, sconv=state_sconv)
    yp, sp = _trunk(x_prompt.reshape(bsz * seq, d),
                    lambda i: _layer_mods(mods, i, 0, bsz, True), True, bsz, seq, None, P)
    ys, ss = _trunk(x_sample.reshape(bn, d),
                    lambda i: _layer_mods(mods, i, bsz, bsz + bn, False), False, bn, 1, st, P)
    names = ('conf', 'k', 'v', 'ssm', 'gconv', 'sconv')
    outs = [yp.reshape(bsz, seq, d), ys.reshape(bn, 1, d)]
    for nm in names:
        outs += [sp[nm][None], ss[nm][None]]
    return tuple(outs)
```

```python
import functools
import math

import jax
import jax.numpy as jnp
from jax import lax
from jax.experimental import pallas as pl
from jax.experimental.pallas import tpu as pltpu

F32 = jnp.float32
BF16 = jnp.bfloat16

D_MODEL = 1024
DEPTH = 4
D_FF = 4 * D_MODEL
CONF_K = 31
HEAD_DIM = 64
N_HEADS = 16
N_KV = 4
WINDOW = 128
ROT_DIM = 16
ROPE_THETA = 500000.0
PAST_LEN = 8192
GDN_DK = 128
GDN_DV = 128
GDN_HK = 8
GDN_HV = 16
GDN_QK_W = GDN_HK * GDN_DK
GDN_V_W = GDN_HV * GDN_DV
GDN_CONV_DIM = 2 * GDN_QK_W + GDN_V_W
GDN_CONV_K = 4
GDN_CHUNK = 64
SCONV_K = 3
RMS_EPS = 1e-6
LN_EPS = 1e-5
L2_EPS = 1e-6

LANES = 128
SUBLANES = 8
HALO_MIN = 16
VMEM_LIMIT = 56 * 1024 * 1024


def _cparams(sem):
    return pltpu.CompilerParams(dimension_semantics=sem, vmem_limit_bytes=VMEM_LIMIT)


def _rows2d(ref):
    return ref[0] if len(ref.shape) == 3 else ref[...]


def _mod_spec(arr, tm, rows_per_seq):
    d = arr.shape[-1]
    if arr.ndim == 3:
        return pl.BlockSpec((1, 1, d), lambda i, *_: ((i * tm) // rows_per_seq, 0, 0))
    return pl.BlockSpec((tm, d), lambda i, *_: (i, 0))


def _const_spec(arr):
    nd = arr.ndim
    return pl.BlockSpec(arr.shape, lambda *_: (0,) * nd)


def _resident_spec(arr):
    nd = arr.ndim
    return pl.BlockSpec(arr.shape, lambda *_: (0,) * nd, pipeline_mode=pl.Buffered(1))


def _normmod(x, g, sh, sc):
    y = x * lax.rsqrt(jnp.mean(x * x, axis=-1, keepdims=True) + RMS_EPS)
    y = y * g
    return y * (1.0 + sc) + sh


def _sigmoid(x):
    return jax.nn.sigmoid(x)


def _silu(x):
    return x * jax.nn.sigmoid(x)


def _dot(a, b):
    return jnp.dot(a.astype(BF16), b.astype(BF16), preferred_element_type=F32)


def _ada_kernel(c_ref, w_ref, b_ref, o_ref):
    cm = _silu(c_ref[...])
    o_ref[0] = _dot(cm, w_ref[0]) + b_ref[0]


def ada_all(c, w_ada, b_ada, *, tn=1024):
    depth, d, n = w_ada.shape
    r = c.shape[0]
    return pl.pallas_call(
        _ada_kernel,
        out_shape=jax.ShapeDtypeStruct((depth, r, n), F32),
        grid=(depth, n // tn),
        in_specs=[pl.BlockSpec((r, d), lambda l, j: (0, 0)),
                  pl.BlockSpec((1, d, tn), lambda l, j: (l, 0, j)),
                  pl.BlockSpec((1, 1, tn), lambda l, j: (l, 0, j))],
        out_specs=pl.BlockSpec((1, r, tn), lambda l, j: (l, 0, j)),
        compiler_params=_cparams(("parallel", "parallel")),
        name="ada_all",
    )(c, w_ada, b_ada.reshape(depth, 1, n))


def _nmm_kernel(*refs, splits, has_bias, n_extra, n_out, epilogue, tn, side_fn, n_side_in,
                n_side_out):
    x_ref, g_ref, sh_ref, sc_ref, w_ref = refs[:5]
    p = 5
    b_ref = None
    if has_bias:
        b_ref = refs[p]; p += 1
    e_refs = refs[p:p + n_extra]; p += n_extra
    si_refs = refs[p:p + n_side_in]; p += n_side_in
    o_refs = refs[p:p + n_out]; p += n_out
    so_refs = refs[p:p + n_side_out]
    h = _normmod(x_ref[...], g_ref[...], _rows2d(sh_ref), _rows2d(sc_ref)).astype(BF16)
    if side_fn is not None:
        side_fn(h, si_refs, so_refs)
    extras = [r[...] for r in e_refs]
    ng = w_ref.shape[1] // splits
    for j in range(ng // tn):
        accs = []
        for s in range(splits):
            c0 = s * ng + j * tn
            a = jnp.dot(h, w_ref[:, c0:c0 + tn], preferred_element_type=F32)
            if has_bias:
                a = a + b_ref[:, c0:c0 + tn]
            accs.append(a)
        outs = epilogue(accs, extras, j * tn)
        for o_ref, o in zip(o_refs, outs):
            o_ref[:, j * tn:(j + 1) * tn] = o.astype(o_ref.dtype)


def nmm(x, gnorm, sh, sc, w, bias, *, splits, n_out, epilogue, tm, tn, rows_per_seq,
        out_dtype, extras=(), extra_specs=(), side=None, name):
    m, d = x.shape
    side_fn, side_in, side_in_specs, side_out_shapes, side_out_specs = (
        side if side is not None else (None, (), (), (), ()))
    n = w.shape[1]
    ng = n // splits
    in_specs = [pl.BlockSpec((tm, d), lambda i: (i, 0)),
                pl.BlockSpec((1, d), lambda i: (0, 0)),
                _mod_spec(sh, tm, rows_per_seq), _mod_spec(sc, tm, rows_per_seq),
                _resident_spec(w)]
    args = [x, gnorm.reshape(1, d), sh, sc, w]
    if bias is not None:
        b2 = bias.reshape(1, n)
        in_specs.append(_const_spec(b2))
        args.append(b2)
    in_specs += list(extra_specs) + list(side_in_specs)
    args += list(extras) + list(side_in)
    kern = functools.partial(_nmm_kernel, splits=splits, has_bias=bias is not None,
                             n_extra=len(extras), n_out=n_out, epilogue=epilogue, tn=tn,
                             side_fn=side_fn, n_side_in=len(side_in),
                             n_side_out=len(side_out_shapes))
    outs = pl.pallas_call(
        kern,
        out_shape=[jax.ShapeDtypeStruct((m, ng), out_dtype)] * n_out + list(side_out_shapes),
        grid=(m // tm,),
        in_specs=in_specs,
        out_specs=[pl.BlockSpec((tm, ng), lambda i: (i, 0))] * n_out + list(side_out_specs),
        compiler_params=_cparams(("parallel",)),
        name=name,
    )(*args)
    return outs


def _ep_plain(accs, extras, col0):
    return [accs[0]]


def _ep_glu(accs, extras, col0):
    return [accs[0] * _sigmoid(accs[1])]


def _ep_sconv(accs, extras, col0):
    return [accs[0], accs[1] * accs[2]]


def _mlp_kernel(*refs, tf, final_norm):
    x_ref, g_ref, sh_ref, sc_ref, gate_ref, wu_ref, wd_ref = refs[:7]
    p = 7
    gf_ref = None
    if final_norm:
        gf_ref = refs[p]; p += 1
    o_ref = refs[p]
    x = x_ref[...]
    h = _normmod(x, g_ref[...], _rows2d(sh_ref), _rows2d(sc_ref)).astype(BF16)
    ff = wu_ref.shape[1]
    acc = jnp.zeros(x.shape, F32)
    for c in range(ff // tf):
        a = jnp.dot(h, wu_ref[:, c * tf:(c + 1) * tf], preferred_element_type=F32)
        a = jnp.square(jnp.maximum(a, 0.0)).astype(BF16)
        acc = acc + jnp.dot(a, wd_ref[c * tf:(c + 1) * tf, :], preferred_element_type=F32)
    y = x + _rows2d(gate_ref) * acc
    if final_norm:
        y = y * lax.rsqrt(jnp.mean(y * y, axis=-1, keepdims=True) + RMS_EPS) * gf_ref[...]
    o_ref[...] = y


def mlp_block(x, gnorm, sh, sc, gate, w_up, w_down, *, tm, rows_per_seq, tf=512,
              norm_final=None):
    m, d = x.shape
    in_specs = [pl.BlockSpec((tm, d), lambda i: (i, 0)),
                pl.BlockSpec((1, d), lambda i: (0, 0)),
                _mod_spec(sh, tm, rows_per_seq), _mod_spec(sc, tm, rows_per_seq),
                _mod_spec(gate, tm, rows_per_seq),
                _resident_spec(w_up), _resident_spec(w_down)]
    args = [x, gnorm.reshape(1, d), sh, sc, gate, w_up, w_down]
    if norm_final is not None:
        in_specs.append(pl.BlockSpec((1, d), lambda i: (0, 0)))
        args.append(norm_final.reshape(1, d))
    return pl.pallas_call(
        functools.partial(_mlp_kernel, tf=tf, final_norm=norm_final is not None),
        out_shape=jax.ShapeDtypeStruct((m, d), F32),
        grid=(m // tm,),
        in_specs=in_specs,
        out_specs=pl.BlockSpec((tm, d), lambda i: (i, 0)),
        compiler_params=_cparams(("parallel",)),
        name="mlp_block",
    )(*args)


def _pro_id(mains, pars):
    return mains[0]


def _pro_conf(mains, pars):
    bdw, lng, lnb = pars
    y = mains[0] + bdw
    mu = jnp.mean(y, axis=-1, keepdims=True)
    var = jnp.mean(jnp.square(y - mu), axis=-1, keepdims=True)
    yn = (y - mu) * lax.rsqrt(var + LN_EPS) * lng + lnb
    return _silu(yn)


def _pro_mul(mains, pars):
    return mains[1].astype(F32) * mains[0]


def _pro_gdn(mains, pars):
    o, z = [a.astype(F32) for a in mains]
    ng = pars[0]
    outs = []
    for hh in range(GDN_HV):
        oh = o[:, hh * GDN_DV:(hh + 1) * GDN_DV]
        yh = oh * lax.rsqrt(jnp.mean(oh * oh, axis=-1, keepdims=True) + RMS_EPS) * ng
        outs.append(yh * _silu(z[:, hh * GDN_DV:(hh + 1) * GDN_DV]))
    return jnp.concatenate(outs, axis=-1)


def _tail_compute(a, w_ref, b_ref, x_ref, gate_ref, o_ref):
    out = jnp.dot(a.astype(BF16), w_ref[...], preferred_element_type=F32)
    if b_ref is not None:
        out = out + b_ref[...]
    o_ref[...] = x_ref[...] + _rows2d(gate_ref) * out


def _tail_kernel(*refs, n_main, n_par, has_bias, prologue):
    mains = [r[...] for r in refs[:n_main]]
    p = n_main
    pars = [r[...] for r in refs[p:p + n_par]]; p += n_par
    w_ref = refs[p]; p += 1
    b_ref = None
    if has_bias:
        b_ref = refs[p]; p += 1
    x_ref, gate_ref, o_ref = refs[p:p + 3]
    _tail_compute(prologue(mains, pars), w_ref, b_ref, x_ref, gate_ref, o_ref)


def _main_spec(a, tm):
    if isinstance(a, tuple):
        arr, width, cb = a
        return arr, pl.BlockSpec((tm, width), lambda i, cb=cb: (i, cb))
    return a, pl.BlockSpec((tm, a.shape[1]), lambda i: (i, 0))


def tail(mains, pars, w, bias, x, gate, *, prologue, tm, rows_per_seq, name):
    m, d = x.shape
    in_specs, args = [], []
    for a in mains:
        arr, spec = _main_spec(a, tm)
        in_specs.append(spec)
        args.append(arr)
    for prm in pars:
        prm = prm.reshape(1, -1)
        in_specs.append(_const_spec(prm))
        args.append(prm)
    in_specs.append(_resident_spec(w)); args.append(w)
    if bias is not None:
        b2 = bias.reshape(1, -1)
        in_specs.append(_const_spec(b2)); args.append(b2)
    in_specs += [pl.BlockSpec((tm, d), lambda i: (i, 0)), _mod_spec(gate, tm, rows_per_seq)]
    args += [x, gate]
    return pl.pallas_call(
        functools.partial(_tail_kernel, n_main=len(mains), n_par=len(pars),
                          has_bias=bias is not None, prologue=prologue),
        out_shape=jax.ShapeDtypeStruct((m, d), F32),
        grid=(m // tm,),
        in_specs=in_specs,
        out_specs=pl.BlockSpec((tm, d), lambda i: (i, 0)),
        compiler_params=_cparams(("parallel",)),
        name=name,
    )(*args)


def _dwconv_tile(win_ref, w_ref, k_taps, base, r0, rows, l0, lc):
    acc = None
    for k in range(k_taps):
        t = w_ref[k:k + 1, l0:l0 + lc] * win_ref[r0 + base + k:r0 + base + k + rows, l0:l0 + lc]
        acc = t if acc is None else acc + t
    return acc


def _fill_window(win, cur_ref, halo_ref, halo, tiles_per_seq):
    first = (pl.program_id(0) % tiles_per_seq) == 0
    win[0:halo, :] = jnp.where(first, 0.0, halo_ref[...].astype(F32))
    win[halo:, :] = cur_ref[...].astype(F32)


def _conv_tail_kernel(*refs, k_taps, halo, tt, tiles_per_seq, rc, lc, n_main, n_par,
                      has_bias, prologue):
    u_ref, halo_ref, wc_ref = refs[:3]
    p = 3
    main_refs = refs[p:p + n_main]; p += n_main
    par_refs = refs[p:p + n_par]; p += n_par
    w_ref = refs[p]; p += 1
    b_ref = None
    if has_bias:
        b_ref = refs[p]; p += 1
    x_ref, gate_ref, o_ref, win, ybuf = refs[p:p + 5]
    _fill_window(win, u_ref, halo_ref, halo, tiles_per_seq)
    c = u_ref.shape[1]
    base = halo - (k_taps - 1)
    for r in range(tt // rc):
        for l in range(c // lc):
            ybuf[r * rc:(r + 1) * rc, l * lc:(l + 1) * lc] = _dwconv_tile(
                win, wc_ref, k_taps, base, r * rc, rc, l * lc, lc)
    mains = [ybuf[...]] + [r[...] for r in main_refs]
    pars = [r[...] for r in par_refs]
    _tail_compute(prologue(mains, pars), w_ref, b_ref, x_ref, gate_ref, o_ref)


def conv_tail(u, wconv, mains, pars, w, bias, x, gate, *, halo, tt, rows_per_seq,
              prologue, name, rc=32, lc=256):
    m, d = x.shape
    c = u.shape[1]
    k_taps = wconv.shape[0]
    hb = tt // halo
    in_specs = [pl.BlockSpec((tt, c), lambda i: (i, 0)),
                pl.BlockSpec((halo, c), lambda i: (jnp.maximum(i * hb - 1, 0), 0)),
                _const_spec(wconv)]
    args = [u, u, wconv]
    for a in mains:
        arr, spec = _main_spec(a, tt)
        in_specs.append(spec)
        args.append(arr)
    for prm in pars:
        prm = prm.reshape(1, -1)
        in_specs.append(_const_spec(prm)); args.append(prm)
    in_specs.append(_resident_spec(w)); args.append(w)
    if bias is not None:
        b2 = bias.reshape(1, -1)
        in_specs.append(_const_spec(b2)); args.append(b2)
    in_specs += [pl.BlockSpec((tt, d), lambda i: (i, 0)), _mod_spec(gate, tt, rows_per_seq)]
    args += [x, gate]
    kern = functools.partial(
        _conv_tail_kernel, k_taps=k_taps, halo=halo, tt=tt, tiles_per_seq=rows_per_seq // tt,
        rc=rc, lc=lc, n_main=len(mains), n_par=len(pars), has_bias=bias is not None,
        prologue=prologue)
    return pl.pallas_call(
        kern,
        out_shape=jax.ShapeDtypeStruct((m, d), F32),
        grid=(m // tt,),
        in_specs=in_specs,
        out_specs=pl.BlockSpec((tt, d), lambda i: (i, 0)),
        scratch_shapes=[pltpu.VMEM((halo + tt, c), F32), pltpu.VMEM((tt, c), F32)],
        compiler_params=_cparams(("parallel",)),
        name=name,
    )(*args)


def _conv_step_kernel(st_ref, u_ref, w_ref, y_ref, ns_ref, *, k_taps, c, lc, act):
    for l in range(c // lc):
        u = u_ref[:, l * lc:(l + 1) * lc]
        acc = w_ref[k_taps - 1:k_taps, l * lc:(l + 1) * lc] * u
        for k in range(k_taps - 1):
            s = st_ref[:, k * c + l * lc:k * c + (l + 1) * lc]
            acc = acc + w_ref[k:k + 1, l * lc:(l + 1) * lc] * s
            if k >= 1:
                ns_ref[:, (k - 1) * c + l * lc:(k - 1) * c + (l + 1) * lc] = s
        ns_ref[:, (k_taps - 2) * c + l * lc:(k_taps - 2) * c + (l + 1) * lc] = u
        y_ref[:, l * lc:(l + 1) * lc] = act(acc)


def conv_step(state, u, wconv, *, act=lambda v: v, bb=32, name):
    bn, km1, c = state.shape
    k_taps = km1 + 1
    bb = min(bb, bn)
    st2 = state.reshape(bn, km1 * c)
    y, ns = pl.pallas_call(
        functools.partial(_conv_step_kernel, k_taps=k_taps, c=c, lc=min(c, 1024), act=act),
        out_shape=[jax.ShapeDtypeStruct((bn, c), F32),
                   jax.ShapeDtypeStruct((bn, km1 * c), F32)],
        grid=(bn // bb,),
        in_specs=[pl.BlockSpec((bb, km1 * c), lambda i: (i, 0)),
                  pl.BlockSpec((bb, c), lambda i: (i, 0)),
                  _const_spec(wconv)],
        out_specs=[pl.BlockSpec((bb, c), lambda i: (i, 0)),
                   pl.BlockSpec((bb, km1 * c), lambda i: (i, 0))],
        compiler_params=_cparams(("parallel",)),
        name=name,
    )(st2, u, wconv)
    return y, ns.reshape(bn, km1, c)


QKV_TN = 512
N_ROPE_COLS = (N_HEADS + N_KV) * HEAD_DIM


def rope_tables(pos):
    half = ROT_DIM // 2
    inv = jnp.power(jnp.float32(ROPE_THETA), -jnp.arange(half, dtype=F32) * 2.0 / ROT_DIM)
    ang = pos.astype(F32)[:, None] * inv[None, :]
    cos, sin = jnp.cos(ang), jnp.sin(ang)
    lane = jnp.arange(LANES) % HEAD_DIM
    idx = lane % half
    cosf = jnp.where(lane < ROT_DIM, cos[:, idx], 1.0)
    sina = jnp.where(lane < half, -sin[:, idx], 0.0)
    sinb = jnp.where((lane >= half) & (lane < ROT_DIM), sin[:, idx], 0.0)
    return cosf, sina, sinb


def _ep_rope(accs, extras, col0):
    a = accs[0]
    cosf, sina, sinb = extras
    half = ROT_DIM // 2
    outs = []
    for g in range(a.shape[1] // LANES):
        xg = a[:, g * LANES:(g + 1) * LANES]
        if col0 + g * LANES < N_ROPE_COLS:
            xg = (xg * cosf + pltpu.roll(xg, LANES - half, 1) * sina
                  + pltpu.roll(xg, half, 1) * sinb)
        outs.append(xg)
    return [jnp.concatenate(outs, axis=-1)]


def _dot_nt(a, b):
    return lax.dot_general(a, b, (((1,), (1,)), ((), ())), preferred_element_type=F32)


def _dot_tn(a, b):
    return lax.dot_general(a, b, (((0,), (0,)), ((), ())), preferred_element_type=F32)


def _swa_kernel(sink_ref, q_ref, kc_ref, kp_ref, vc_ref, vp_ref, o_ref):
    n = pl.program_id(1)
    w = WINDOW
    lane = lax.broadcasted_iota(jnp.int32, (1, LANES), 1)
    lo = lane < HEAD_DIM
    row = lax.broadcasted_iota(jnp.int32, (2 * w, 1), 0)
    qi = row % w
    kj = lax.broadcasted_iota(jnp.int32, (1, 2 * w), 1)
    band = (kj >= qi) & (kj <= qi + w) & ((kj >= w) | (n > 0))
    top = row < w
    scale = HEAD_DIM ** -0.5
    for h in range(N_KV):
        cg = (h // 2) * LANES
        kg = jnp.concatenate([kp_ref[:, cg:cg + LANES], kc_ref[:, cg:cg + LANES]], axis=0)
        vg = jnp.concatenate([vp_ref[:, cg:cg + LANES], vc_ref[:, cg:cg + LANES]], axis=0)
        kr = pltpu.roll(kg, HEAD_DIM, 1)
        vr = pltpu.roll(vg, HEAD_DIM, 1)
        if h % 2 == 0:
            k_lo, k_hi = jnp.where(lo, kg, 0.0), jnp.where(lo, 0.0, kr)
            v_lo, v_hi = jnp.where(lo, vg, 0.0), jnp.where(lo, 0.0, vr)
        else:
            k_lo, k_hi = jnp.where(lo, kr, 0.0), jnp.where(lo, 0.0, kg)
            v_lo, v_hi = jnp.where(lo, vr, 0.0), jnp.where(lo, 0.0, vg)
        c0 = h * 4 * HEAD_DIM
        q2 = jnp.concatenate([q_ref[:, c0:c0 + LANES], q_ref[:, c0 + LANES:c0 + 2 * LANES]],
                             axis=0).astype(BF16)
        o2 = None
        for par, kpad, vpad in ((0, k_lo, v_lo), (1, k_hi, v_hi)):
            s = _dot_nt(q2, kpad.astype(BF16)) * scale
            s = jnp.where(band, s, -jnp.inf)
            sink = jnp.where(top, sink_ref[4 * h + par], sink_ref[4 * h + 2 + par])
            m = jnp.maximum(jnp.max(s, axis=-1, keepdims=True), sink)
            p = jnp.exp(s - m)
            den = jnp.sum(p, axis=-1, keepdims=True) + jnp.exp(sink - m)
            part = jnp.dot(p.astype(BF16), vpad.astype(BF16), preferred_element_type=F32) / den
            o2 = part if o2 is None else o2 + part
        o_ref[:, c0:c0 + LANES] = o2[:w].astype(o_ref.dtype)
        o_ref[:, c0 + LANES:c0 + 2 * LANES] = o2[w:].astype(o_ref.dtype)


def swa_prompt(qkv, sinks, *, bsz, seq):
    nb = seq // WINDOW
    dq = N_HEADS * HEAD_DIM
    dkv = N_KV * HEAD_DIM
    kcol, vcol = dq // dkv, dq // dkv + 1
    cur = lambda b, n: b * nb + n
    prev = lambda b, n: b * nb + jnp.maximum(n - 1, 0)
    return pl.pallas_call(
        _swa_kernel,
        out_shape=jax.ShapeDtypeStruct((bsz * seq, dq), BF16),
        grid=(bsz, nb),
        in_specs=[pl.BlockSpec(memory_space=pltpu.SMEM),
                  pl.BlockSpec((WINDOW, dq), lambda b, n: (cur(b, n), 0)),
                  pl.BlockSpec((WINDOW, dkv), lambda b, n: (cur(b, n), kcol)),
                  pl.BlockSpec((WINDOW, dkv), lambda b, n: (prev(b, n), kcol)),
                  pl.BlockSpec((WINDOW, dkv), lambda b, n: (cur(b, n), vcol)),
                  pl.BlockSpec((WINDOW, dkv), lambda b, n: (prev(b, n), vcol))],
        out_specs=pl.BlockSpec((WINDOW, dq), lambda b, n: (cur(b, n), 0)),
        compiler_params=_cparams(("parallel", "parallel")),
        name="swa_prompt",
    )(sinks, qkv, qkv, qkv, qkv, qkv)


def _swa_decode_kernel(sink_ref, q_ref, kn_ref, vn_ref, kc_ref, vc_ref, o_ref, ko_ref, vo_ref,
                       *, bb):
    w = WINDOW
    dkv = N_KV * HEAD_DIM
    lane_head = lax.broadcasted_iota(jnp.int32, (N_KV, dkv), 1) // HEAD_DIM
    row_head = lax.broadcasted_iota(jnp.int32, (N_KV, dkv), 0)
    own = lane_head == row_head
    sinks = sink_ref[...]
    scale = HEAD_DIM ** -0.5

    def body(b, carry):
        qb = q_ref[b]
        qblk = jnp.concatenate(
            [jnp.where(own, jnp.broadcast_to(qb[g:g + 1, :], (N_KV, dkv)), 0.0)
             for g in range(N_HEADS // N_KV)], axis=0)
        kc = kc_ref[b]
        vc = vc_ref[b]
        kn = kn_ref[b]
        vn = vn_ref[b]
        qb16 = qblk.astype(BF16)
        s_c = _dot_nt(qb16, kc.astype(BF16)) * scale
        s_n = jnp.sum(qblk * kn, axis=-1, keepdims=True) * scale
        m = jnp.maximum(jnp.maximum(jnp.max(s_c, axis=-1, keepdims=True), s_n), sinks)
        p_c = jnp.exp(s_c - m)
        p_n = jnp.exp(s_n - m)
        den = jnp.sum(p_c, axis=-1, keepdims=True) + p_n + jnp.exp(sinks - m)
        p_c = p_c / den
        p_n = p_n / den
        o = (jnp.dot(p_c.astype(BF16), vc.astype(BF16), preferred_element_type=F32)
             + p_n * vn)
        for g in range(N_HEADS // N_KV):
            og = jnp.sum(jnp.where(own, o[g * N_KV:(g + 1) * N_KV, :], 0.0), axis=0,
                         keepdims=True)
            o_ref[b, g:g + 1, :] = og
        ko_ref[b, 0:w - 1, :] = kc[1:w, :]
        ko_ref[b, w - 1:w, :] = kn
        vo_ref[b, 0:w - 1, :] = vc[1:w, :]
        vo_ref[b, w - 1:w, :] = vn
        return carry

    lax.fori_loop(0, bb, body, 0)


def swa_decode(qg, k_new, v_new, k_cache, v_cache, sinks_gh, *, bb=8):
    bn = qg.shape[0]
    dkv = N_KV * HEAD_DIM
    ng = N_HEADS // N_KV
    return pl.pallas_call(
        functools.partial(_swa_decode_kernel, bb=bb),
        out_shape=[jax.ShapeDtypeStruct((bn, ng, dkv), F32),
                   jax.ShapeDtypeStruct((bn, WINDOW, dkv), F32),
                   jax.ShapeDtypeStruct((bn, WINDOW, dkv), F32)],
        grid=(bn // bb,),
        in_specs=[_const_spec(sinks_gh),
                  pl.BlockSpec((bb, ng, dkv), lambda i: (i, 0, 0)),
                  pl.BlockSpec((bb, 1, dkv), lambda i: (i, 0, 0)),
                  pl.BlockSpec((bb, 1, dkv), lambda i: (i, 0, 0)),
                  pl.BlockSpec((bb, WINDOW, dkv), lambda i: (i, 0, 0)),
                  pl.BlockSpec((bb, WINDOW, dkv), lambda i: (i, 0, 0))],
        out_specs=[pl.BlockSpec((bb, ng, dkv), lambda i: (i, 0, 0)),
                   pl.BlockSpec((bb, WINDOW, dkv), lambda i: (i, 0, 0)),
                   pl.BlockSpec((bb, WINDOW, dkv), lambda i: (i, 0, 0))],
        compiler_params=_cparams(("parallel",)),
        name="swa_decode",
    )(sinks_gh, qg, k_new.reshape(bn, 1, dkv), v_new.reshape(bn, 1, dkv), k_cache, v_cache)


GDN_BA_PAD = LANES


def _gdn_side(h, si_refs, so_refs):
    wba_ref, wbat_ref = si_refs
    ba_ref, bat_ref = so_refs
    ba_ref[...] = jnp.dot(h, wba_ref[...], preferred_element_type=F32)
    bat_ref[...] = _dot_nt(wbat_ref[...], h)


def _gdn_gates(b, a, alog, dtb):
    x = a + dtb
    sp = jnp.maximum(x, 0.0) + jnp.log1p(jnp.exp(-jnp.abs(x)))
    return -jnp.exp(alog) * sp, _sigmoid(b)


def _l2n(x):
    return x * lax.rsqrt(jnp.sum(x * x, axis=-1, keepdims=True) + L2_EPS)


def _gdn_store_qkv(y, col, rows, q_ref, k_ref, v_ref):
    if col < GDN_QK_W:
        q_ref[rows, col:col + GDN_DK] = (_l2n(y) * (GDN_DK ** -0.5)).astype(q_ref.dtype)
    elif col < 2 * GDN_QK_W:
        k_ref[rows, col - GDN_QK_W:col - GDN_QK_W + GDN_DK] = _l2n(y).astype(k_ref.dtype)
    else:
        v_ref[rows, col - 2 * GDN_QK_W:col - 2 * GDN_QK_W + GDN_DV] = y.astype(v_ref.dtype)


def _gdn_prep_kernel(cur_ref, halo_ref, wc_ref, ba_ref, bat_ref, alr_ref, dtr_ref, alc_ref,
                     dtc_ref, q_ref, k_ref, v_ref, g_ref, b_ref, gt_ref, win, *, tt,
                     tiles_per_seq, rc):
    halo = HALO_MIN
    _fill_window(win, cur_ref, halo_ref, halo, tiles_per_seq)
    base = halo - (GDN_CONV_K - 1)
    for r in range(tt // rc):
        rows = slice(r * rc, (r + 1) * rc)
        for l in range(GDN_CONV_DIM // LANES):
            y = _silu(_dwconv_tile(win, wc_ref, GDN_CONV_K, base, r * rc, rc, l * LANES, LANES))
            _gdn_store_qkv(y, l * LANES, rows, q_ref, k_ref, v_ref)
    ba = ba_ref[...]
    g, beta = _gdn_gates(ba[:, 0:GDN_HV], ba[:, GDN_HV:2 * GDN_HV], alr_ref[...], dtr_ref[...])
    bat = bat_ref[...]
    gt, _ = _gdn_gates(bat[0:GDN_HV, :], bat[GDN_HV:2 * GDN_HV, :], alc_ref[...], dtc_ref[...])
    r = lax.broadcasted_iota(jnp.int32, (tt, tt), 0)
    c = lax.broadcasted_iota(jnp.int32, (tt, tt), 1)
    same = (r // GDN_CHUNK) == (c // GDN_CHUNK)
    g_ref[...] = _dot_hi((same & (r >= c)).astype(F32), g)
    b_ref[...] = beta
    gct = _dot_hi(gt, (same & (r <= c)).astype(F32))
    for ci in range(tt // GDN_CHUNK):
        gt_ref[ci] = gct[:, ci * GDN_CHUNK:(ci + 1) * GDN_CHUNK]


def gdn_prep_prompt(proj, ba, bat, wconv, a_log, dt_bias, *, tt, rows_per_seq, rc=128):
    m = proj.shape[0]
    c = GDN_CONV_DIM
    hb = tt // HALO_MIN
    alr, dtr = a_log.reshape(1, GDN_HV), dt_bias.reshape(1, GDN_HV)
    alc, dtc = a_log.reshape(GDN_HV, 1), dt_bias.reshape(GDN_HV, 1)
    return pl.pallas_call(
        functools.partial(_gdn_prep_kernel, tt=tt, tiles_per_seq=rows_per_seq // tt, rc=rc),
        out_shape=[jax.ShapeDtypeStruct((m, GDN_QK_W), BF16),
                   jax.ShapeDtypeStruct((m, GDN_QK_W), BF16),
                   jax.ShapeDtypeStruct((m, GDN_V_W), BF16),
                   jax.ShapeDtypeStruct((m, GDN_HV), F32),
                   jax.ShapeDtypeStruct((m, GDN_HV), F32),
                   jax.ShapeDtypeStruct((m // GDN_CHUNK, GDN_HV, GDN_CHUNK), F32)],
        grid=(m // tt,),
        in_specs=[pl.BlockSpec((tt, c), lambda i: (i, 0)),
                  pl.BlockSpec((HALO_MIN, c), lambda i: (jnp.maximum(i * hb - 1, 0), 0)),
                  _const_spec(wconv),
                  pl.BlockSpec((tt, GDN_BA_PAD), lambda i: (i, 0)),
                  pl.BlockSpec((2 * GDN_HV, tt), lambda i: (0, i)),
                  _const_spec(alr), _const_spec(dtr), _const_spec(alc), _const_spec(dtc)],
        out_specs=[pl.BlockSpec((tt, GDN_QK_W), lambda i: (i, 0)),
                   pl.BlockSpec((tt, GDN_QK_W), lambda i: (i, 0)),
                   pl.BlockSpec((tt, GDN_V_W), lambda i: (i, 0)),
                   pl.BlockSpec((tt, GDN_HV), lambda i: (i, 0)),
                   pl.BlockSpec((tt, GDN_HV), lambda i: (i, 0)),
                   pl.BlockSpec((tt // GDN_CHUNK, GDN_HV, GDN_CHUNK), lambda i: (i, 0, 0))],
        scratch_shapes=[pltpu.VMEM((HALO_MIN + tt, c), F32)],
        compiler_params=_cparams(("parallel",)),
        name="gdn_prep_prompt",
    )(proj, proj, wconv, ba, bat, alr, dtr, alc, dtc)


def _gdn_prep_step_kernel(y_ref, ba_ref, alr_ref, dtr_ref, q_ref, k_ref, v_ref, g_ref, b_ref):
    rows = slice(None)
    for l in range(GDN_CONV_DIM // LANES):
        _gdn_store_qkv(y_ref[:, l * LANES:(l + 1) * LANES], l * LANES, rows, q_ref, k_ref, v_ref)
    ba = ba_ref[...]
    g, beta = _gdn_gates(ba[:, 0:GDN_HV], ba[:, GDN_HV:2 * GDN_HV], alr_ref[...], dtr_ref[...])
    g_ref[...] = g
    b_ref[...] = beta


def gdn_prep_step(y, ba, a_log, dt_bias):
    bn = y.shape[0]
    return pl.pallas_call(
        _gdn_prep_step_kernel,
        out_shape=[jax.ShapeDtypeStruct((bn, GDN_QK_W), F32),
                   jax.ShapeDtypeStruct((bn, GDN_QK_W), F32),
                   jax.ShapeDtypeStruct((bn, GDN_V_W), F32),
                   jax.ShapeDtypeStruct((bn, GDN_HV), F32),
                   jax.ShapeDtypeStruct((bn, GDN_HV), F32)],
        compiler_params=pltpu.CompilerParams(vmem_limit_bytes=VMEM_LIMIT),
        name="gdn_prep_step",
    )(y, ba, a_log.reshape(1, GDN_HV), dt_bias.reshape(1, GDN_HV))


def _dot_hi(a, b):
    return jnp.dot(a, b, precision=lax.Precision.HIGHEST, preferred_element_type=F32)


GDN_BD_HEADS = 4


def _gdn_chunk_kernel(q_ref, k_ref, v_ref, gc_ref, b_ref, gct_ref, o_ref, sfin_ref, s_scr,
                      *, c, bpb):
    n = pl.program_id(1)

    @pl.when(n == 0)
    def _():
        s_scr[...] = jnp.zeros(s_scr.shape, F32)

    row = lax.broadcasted_iota(jnp.int32, (c, 2 * c), 0)
    lane = lax.broadcasted_iota(jnp.int32, (c, 2 * c), 1)
    colp = lane % c
    lo = lane < c
    lower2 = row >= colp
    strict2 = row > colp
    eye2 = (row == colp).astype(F32)
    rep = GDN_HV // GDN_HK
    ppg = GDN_BD_HEADS // rep
    gw = GDN_BD_HEADS * c
    bdmask16 = (lax.broadcasted_iota(jnp.int32, (gw, gw), 0) // c
                == lax.broadcasted_iota(jnp.int32, (gw, gw), 1) // c).astype(BF16)
    eyecat = jnp.concatenate([eye2] * ppg, axis=1)
    rowc = lax.broadcasted_iota(jnp.int32, (c, gw), 0)
    colc = lax.broadcasted_iota(jnp.int32, (c, gw), 1) % c
    zpad = jnp.zeros((c, GDN_DV), BF16)

    def bd(x):
        return jnp.concatenate([x.astype(BF16)] * GDN_BD_HEADS, axis=0) * bdmask16

    def pair_stack(a0, a1):
        return jnp.concatenate([jnp.concatenate([a0, zpad], axis=1),
                                jnp.concatenate([zpad, a1], axis=1)], axis=0)

    gates = []
    for bi in range(bpb):
        gc = gc_ref[bi]
        glast = gc[c - 1:c, :]
        gates.append((gc, b_ref[bi], gct_ref[bi, 0], jnp.exp(gc), jnp.exp(glast - gc),
                      jnp.exp(glast)))

    pairs = [(bi, hk) for bi in range(bpb) for hk in range(GDN_HK)]
    heads = [(bi, hv) for bi in range(bpb) for hv in range(GDN_HV)]

    kf, qf, a_all = {}, {}, {}
    for bi, hk in pairs:
        kh = k_ref[bi, :, hk * GDN_DK:(hk + 1) * GDN_DK]
        qh = q_ref[bi, :, hk * GDN_DK:(hk + 1) * GDN_DK]
        kf[bi, hk], qf[bi, hk] = kh.astype(F32), qh.astype(F32)
        a_all[bi, hk] = _dot_nt(jnp.concatenate([kh, qh], axis=0),
                                jnp.concatenate([kh, kh], axis=0))

    ks, qs, s_old = {}, {}, {}
    for bi, hv in heads:
        gc, beta, _, eg, _, _ = gates[bi]
        hk = hv // rep
        egc = eg[:, hv:hv + 1]
        kq = jnp.concatenate([kf[bi, hk] * (beta[:, hv:hv + 1] * egc), qf[bi, hk] * egc], axis=0)
        s_old[bi, hv] = s_scr[bi * GDN_HV + hv]
        kqs = jnp.dot(kq.astype(BF16), s_old[bi, hv].astype(BF16), preferred_element_type=F32)
        ks[bi, hv], qs[bi, hv] = kqs[:c], kqs[c:]

    l2, attn2 = {}, {}
    for bi, hk in pairs:
        gc, beta, gct, _, _, _ = gates[bi]
        hv0, hv1 = hk * rep, hk * rep + 1
        gcol2 = jnp.where(lo, gc[:, hv0:hv0 + 1], gc[:, hv1:hv1 + 1])
        grow2 = jnp.concatenate([gct[hv0:hv0 + 1, :], gct[hv1:hv1 + 1, :]], axis=1)
        decay2 = jnp.exp(jnp.where(lower2, gcol2 - grow2, -jnp.inf))
        bcol2 = jnp.where(lo, beta[:, hv0:hv0 + 1], beta[:, hv1:hv1 + 1])
        l2[bi, hk] = jnp.where(strict2, bcol2 * a_all[bi, hk][:c] * decay2, 0.0)
        attn2[bi, hk] = (a_all[bi, hk][c:] * decay2).astype(BF16)

    groups = [(bi, g) for bi in range(bpb) for g in range(GDN_HK // ppg)]
    lcat = {(bi, g): jnp.concatenate([l2[bi, g * ppg + pi] for pi in range(ppg)], axis=1)
            for bi, g in groups}

    def e_mask(s):
        return (rowc // (2 * s) == colc // (2 * s)) & (rowc // s != colc // s)

    m1 = e_mask(1)
    t = {key: eyecat - jnp.where(m1, lcat[key], 0.0) for key in groups}
    s = 2
    while s < c:
        ms = e_mask(s)
        x = {key: jnp.dot(jnp.where(ms, lcat[key], 0.0).astype(BF16), bd(t[key]),
                          preferred_element_type=F32) for key in groups}
        t = {key: t[key] - jnp.dot(t[key].astype(BF16), bd(x[key]), preferred_element_type=F32)
             for key in groups}
        s *= 2

    vn = {}
    for bi, hk in pairs:
        _, beta, _, _, _, _ = gates[bi]
        ys = []
        for hv in (hk * rep, hk * rep + 1):
            vh = v_ref[bi, :, hv * GDN_DV:(hv + 1) * GDN_DV].astype(F32)
            ys.append((vh * beta[:, hv:hv + 1] - ks[bi, hv]).astype(BF16))
        pi = hk % ppg
        tp = t[bi, hk // ppg][:, pi * 2 * c:(pi + 1) * 2 * c].astype(BF16)
        vn[bi, hk] = jnp.dot(tp, pair_stack(ys[0], ys[1]),
                             preferred_element_type=F32).astype(BF16)

    for bi, hk in pairs:
        hv0 = hk * rep
        v2 = vn[bi, hk]
        o2 = jnp.dot(attn2[bi, hk], pair_stack(v2[:, :GDN_DV], v2[:, GDN_DV:]),
                     preferred_element_type=F32)
        o_ref[bi, :, hv0 * GDN_DV:(hv0 + rep) * GDN_DV] = (o2 + jnp.concatenate(
            [qs[bi, hv0], qs[bi, hv0 + 1]], axis=1)).astype(o_ref.dtype)

    for bi, hv in heads:
        _, _, _, _, ek, egl = gates[bi]
        hk = hv // rep
        kd = (kf[bi, hk] * ek[:, hv:hv + 1]).astype(BF16)
        v16 = vn[bi, hk][:, (hv % rep) * GDN_DV:(hv % rep + 1) * GDN_DV]
        s_scr[bi * GDN_HV + hv] = s_old[bi, hv] * egl[:, hv:hv + 1] + _dot_tn(kd, v16)

    @pl.when(n == pl.num_programs(1) - 1)
    def _():
        for bi in range(bpb):
            sfin_ref[bi] = s_scr[bi * GDN_HV:(bi + 1) * GDN_HV]


def gdn_chunked(q, k, v, gc, beta, gct, *, bsz, seq, c=GDN_CHUNK, bpb=2):
    nc = seq // c
    r3 = lambda a: a.reshape(bsz, seq, a.shape[-1])
    idx = lambda b, n: (b, n, 0)
    o, s_fin = pl.pallas_call(
        functools.partial(_gdn_chunk_kernel, c=c, bpb=bpb),
        out_shape=[jax.ShapeDtypeStruct((bsz, seq, GDN_V_W), BF16),
                   jax.ShapeDtypeStruct((bsz, GDN_HV, GDN_DK, GDN_DV), F32)],
        grid=(bsz // bpb, nc),
        in_specs=[pl.BlockSpec((bpb, c, GDN_QK_W), idx), pl.BlockSpec((bpb, c, GDN_QK_W), idx),
                  pl.BlockSpec((bpb, c, GDN_V_W), idx), pl.BlockSpec((bpb, c, GDN_HV), idx),
                  pl.BlockSpec((bpb, c, GDN_HV), idx),
                  pl.BlockSpec((bpb, 1, GDN_HV, c), lambda b, n: (b, n, 0, 0))],
        out_specs=[pl.BlockSpec((bpb, c, GDN_V_W), idx),
                   pl.BlockSpec((bpb, GDN_HV, GDN_DK, GDN_DV), lambda b, n: (b, 0, 0, 0))],
        scratch_shapes=[pltpu.VMEM((bpb * GDN_HV, GDN_DK, GDN_DV), F32)],
        compiler_params=_cparams(("parallel", "arbitrary")),
        name="gdn_chunked",
    )(r3(q), r3(k), r3(v), r3(gc), r3(beta), gct.reshape(bsz, nc, GDN_HV, c))
    return o.reshape(bsz * seq, GDN_V_W), s_fin


def _gdn_decode_kernel(q_ref, k_ref, v_ref, g_ref, b_ref, s_ref, o_ref, so_ref, qt_scr, kt_scr):
    b = pl.program_id(0)

    @pl.when(b == 0)
    def _():
        for hk in range(GDN_HK):
            qt_scr[hk] = q_ref[:, hk * GDN_DK:(hk + 1) * GDN_DK].T
            kt_scr[hk] = k_ref[:, hk * GDN_DK:(hk + 1) * GDN_DK].T

    pick = lax.broadcasted_iota(jnp.int32, (1, q_ref.shape[0]), 1) == b
    grow = g_ref[0]
    brow = b_ref[0]
    rep = GDN_HV // GDN_HK
    for hv in range(GDN_HV):
        hk = hv // rep
        kcol = jnp.sum(jnp.where(pick, kt_scr[hk], 0.0), axis=1, keepdims=True)
        qcol = jnp.sum(jnp.where(pick, qt_scr[hk], 0.0), axis=1, keepdims=True)
        s1 = s_ref[0, hv] * jnp.exp(grow[:, hv:hv + 1])
        kv = jnp.sum(kcol * s1, axis=0, keepdims=True)
        vrow = v_ref[0, :, hv * GDN_DV:(hv + 1) * GDN_DV]
        s2 = s1 + kcol * ((vrow - kv) * brow[:, hv:hv + 1])
        so_ref[0, hv] = s2
        o_ref[0, :, hv * GDN_DV:(hv + 1) * GDN_DV] = jnp.sum(qcol * s2, axis=0, keepdims=True)


def gdn_decode(q, k, v, g, beta, state):
    bn = q.shape[0]
    sspec = pl.BlockSpec((1, GDN_HV, GDN_DK, GDN_DV), lambda b: (b, 0, 0, 0))
    rowspec = lambda width: pl.BlockSpec((1, 1, width), lambda b: (b, 0, 0))
    o, s_new = pl.pallas_call(
        _gdn_decode_kernel,
        out_shape=[jax.ShapeDtypeStruct((bn, 1, GDN_V_W), F32),
                   jax.ShapeDtypeStruct(state.shape, F32)],
        grid=(bn,),
        in_specs=[_const_spec(q), _const_spec(k), rowspec(GDN_V_W), rowspec(GDN_HV),
                  rowspec(GDN_HV), sspec],
        out_specs=[rowspec(GDN_V_W), sspec],
        scratch_shapes=[pltpu.VMEM((GDN_HK, GDN_DK, bn), F32),
                        pltpu.VMEM((GDN_HK, GDN_DK, bn), F32)],
        compiler_params=_cparams(("arbitrary",)),
        name="gdn_decode",
    )(q, k, v.reshape(bn, 1, GDN_V_W), g.reshape(bn, 1, GDN_HV), beta.reshape(bn, 1, GDN_HV),
      state)
    return o.reshape(bn, GDN_V_W), s_new


N_MIXERS = 4
TM_PROMPT = 512
TT_PROMPT = 256
ADA_ROWS_PAD = SUBLANES


def _layer_mods(mods, i, lo, hi, per_seq):
    d = D_MODEL
    out = []
    for k in range(6):
        v = mods[i, lo:hi, k * d:(k + 1) * d]
        out.append(v.reshape(hi - lo, 1, d) if per_seq else v)
    return out


def _gdn_weights(w_in):
    n_main = GDN_CONV_DIM + GDN_V_W
    w_ba = w_in[:, n_main:n_main + 2 * GDN_HV]
    w_ba_pad = jnp.pad(w_ba, ((0, 0), (0, GDN_BA_PAD - 2 * GDN_HV)))
    return w_in[:, :n_main].astype(BF16), w_ba_pad.astype(BF16), w_ba.T.astype(BF16)


def _gdn_inproj(x, gnorm, sh, sc, w_main, w_ba, w_bat, *, tm, rows_per_seq, out_dtype):
    m = x.shape[0]
    side = (_gdn_side, (w_ba, w_bat), (_const_spec(w_ba), _const_spec(w_bat)),
            (jax.ShapeDtypeStruct((m, GDN_BA_PAD), F32),
             jax.ShapeDtypeStruct((2 * GDN_HV, m), F32)),
            (pl.BlockSpec((tm, GDN_BA_PAD), lambda i: (i, 0)),
             pl.BlockSpec((2 * GDN_HV, tm), lambda i: (0, i))))
    return nmm(x, gnorm, sh, sc, w_main, None, splits=1, n_out=1, epilogue=_ep_plain, tm=tm,
               tn=1024, rows_per_seq=rows_per_seq, out_dtype=out_dtype, side=side,
               name="gdn_inproj")


def _trunk(x, mods_of, prompt, bsz, seq, st, P):
    d = D_MODEL
    tm = TM_PROMPT if prompt else x.shape[0]
    rps = seq if prompt else 1
    act = BF16 if prompt else F32
    new = {}
    for i in range(DEPTH):
        mixer, j = i % N_MIXERS, i // N_MIXERS
        sh1, sc1, g1, sh2, sc2, g2 = mods_of(i)
        gn = P['norm_mix'][i]
        if mixer == 0:
            u, = nmm(x, gn, sh1, sc1, P['conf_w_pw1'][j].astype(BF16), P['conf_b_pw1'][j],
                     splits=2, n_out=1, epilogue=_ep_glu, tm=tm, tn=512, rows_per_seq=rps,
                     out_dtype=act, name="conf_pw1")
            pars = [P['conf_b_dw'][j], P['conf_ln_g'][j], P['conf_ln_b'][j]]
            w2, b2 = P['conf_w_pw2'][j].astype(BF16), P['conf_b_pw2'][j]
            if prompt:
                x = conv_tail(u, P['conf_w_dw'][j], [], pars, w2, b2, x, g1, halo=32,
                              tt=TT_PROMPT, rows_per_seq=rps, prologue=_pro_conf,
                              name="conf_conv")
                new['conf'] = u.reshape(bsz, seq, d)[:, seq - (CONF_K - 1):].astype(F32)
            else:
                y, new['conf'] = conv_step(st['conf'][j], u, P['conf_w_dw'][j], name="conf_step")
                x = tail([y], pars, w2, b2, x, g1, prologue=_pro_conf, tm=tm, rows_per_seq=rps,
                         name="conf_tail")
        elif mixer == 1:
            if prompt:
                tabs = rope_tables(jnp.arange(seq, dtype=jnp.int32))
                tspec = pl.BlockSpec((tm, LANES), lambda r: (r % (seq // tm), 0))
            else:
                pos = jnp.full((tm,), PAST_LEN, dtype=jnp.int32)
                tabs = rope_tables(pos)
                tspec = pl.BlockSpec((tm, LANES), lambda r: (0, 0))
            qkv, = nmm(x, gn, sh1, sc1, P['swa_w_qkv'][j].astype(BF16), None, splits=1,
                       n_out=1, epilogue=_ep_rope, tm=tm, tn=QKV_TN, rows_per_seq=rps,
                       out_dtype=F32, extras=tabs, extra_specs=[tspec] * 3, name="swa_qkv")
            dq, dkv = N_HEADS * HEAD_DIM, N_KV * HEAD_DIM
            sinks = P['swa_sinks'][j]
            if prompt:
                o = swa_prompt(qkv, sinks, bsz=bsz, seq=seq)
                last = qkv.reshape(bsz, seq, dq + 2 * dkv)[:, seq - WINDOW:]
                new['k'] = last[..., dq:dq + dkv].reshape(bsz, WINDOW, N_KV, HEAD_DIM)
                new['v'] = last[..., dq + dkv:].reshape(bsz, WINDOW, N_KV, HEAD_DIM)
            else:
                bn = x.shape[0]
                ng = N_HEADS // N_KV
                qg = qkv[:, :dq].reshape(bn, N_KV, ng, HEAD_DIM).transpose(0, 2, 1, 3)
                sinks_gh = sinks.reshape(N_KV, ng).T.reshape(N_HEADS, 1)
                og, ko, vo = swa_decode(qg.reshape(bn, ng, dkv), qkv[:, dq:dq + dkv],
                                        qkv[:, dq + dkv:], st['k'][j].reshape(bn, WINDOW, dkv),
                                        st['v'][j].reshape(bn, WINDOW, dkv), sinks_gh)
                o = og.reshape(bn, ng, N_KV, HEAD_DIM).transpose(0, 2, 1, 3).reshape(bn, dq)
                new['k'] = ko.reshape(bn, WINDOW, N_KV, HEAD_DIM)
                new['v'] = vo.reshape(bn, WINDOW, N_KV, HEAD_DIM)
            x = tail([o], [], P['swa_w_o'][j].astype(BF16), None, x, g1, prologue=_pro_id,
                     tm=tm, rows_per_seq=rps, name="swa_out")
        elif mixer == 2:
            w_main, w_ba, w_bat = _gdn_weights(P['gdn_w_in'][j])
            proj, ba, bat = _gdn_inproj(x, gn, sh1, sc1, w_main, w_ba, w_bat, tm=tm,
                                        rows_per_seq=rps, out_dtype=act)
            wc, al, dtb = P['gdn_w_conv'][j], P['gdn_a_log'][j], P['gdn_dt_bias'][j]
            if prompt:
                q, k, v, g, beta, gt = gdn_prep_prompt(proj, ba, bat, wc, al, dtb, tt=TT_PROMPT,
                                                       rows_per_seq=rps)
                o, new['ssm'] = gdn_chunked(q, k, v, g, beta, gt, bsz=bsz, seq=seq)
                new['gconv'] = proj.reshape(bsz, seq, -1)[:, seq - (GDN_CONV_K - 1):,
                                                          :GDN_CONV_DIM].astype(F32)
            else:
                y, new['gconv'] = conv_step(st['gconv'][j], proj[:, :GDN_CONV_DIM], wc,
                                            act=_silu, name="gdn_conv_step")
                q, k, v, g, beta = gdn_prep_step(y, ba, al, dtb)
                o, new['ssm'] = gdn_decode(q, k, v, g, beta, st['ssm'][j])
            x = tail([o, (proj, GDN_V_W, GDN_CONV_DIM // GDN_V_W)], [P['gdn_norm'][j]],
                     P['gdn_w_o'][j].astype(BF16), None, x, g1, prologue=_pro_gdn, tm=tm,
                     rows_per_seq=rps, name="gdn_out")
        else:
            gb, p = nmm(x, gn, sh1, sc1, P['sconv_w_in'][j].astype(BF16), None, splits=3,
                        n_out=2, epilogue=_ep_sconv, tm=tm, tn=512, rows_per_seq=rps,
                        out_dtype=act, name="sconv_in")
            wo = P['sconv_w_out'][j].astype(BF16)
            if prompt:
                x = conv_tail(p, P['sconv_w_conv'][j], [gb], [], wo, None, x, g1, halo=HALO_MIN,
                              tt=TT_PROMPT, rows_per_seq=rps, prologue=_pro_mul,
                              name="sconv_conv")
                new['sconv'] = p.reshape(bsz, seq, d)[:, seq - (SCONV_K - 1):].astype(F32)
            else:
                y, new['sconv'] = conv_step(st['sconv'][j], p, P['sconv_w_conv'][j],
                                            name="sconv_step")
                x = tail([y, gb], [], wo, None, x, g1, prologue=_pro_mul, tm=tm,
                         rows_per_seq=rps, name="sconv_tail")
        x = mlp_block(x, P['norm_mlp'][i], sh2, sc2, g2, P['w_up'][i].astype(BF16),
                      P['w_down'][i].astype(BF16), tm=tm, rows_per_seq=rps,
                      norm_final=P['norm_final'] if i == DEPTH - 1 else None)
    return x, new


def kernel(x_prompt, x_sample, c_prompt, c_sample, state_conf_conv, cache_swa_k, cache_swa_v,
           state_gdn_ssm, state_gdn_conv, state_sconv, w_ada, b_ada, norm_mix, norm_mlp,
           w_up, w_down, norm_final, conf_w_pw1, conf_b_pw1, conf_w_dw, conf_b_dw, conf_ln_g,
           conf_ln_b, conf_w_pw2, conf_b_pw2, swa_w_qkv, swa_w_o, swa_sinks, gdn_w_in,
           gdn_w_conv, gdn_a_log, gdn_dt_bias, gdn_norm, gdn_w_o, sconv_w_in, sconv_w_conv,
           sconv_w_out):
    P = dict(norm_mix=norm_mix, norm_mlp=norm_mlp, w_up=w_up, w_down=w_down,
             norm_final=norm_final, conf_w_pw1=conf_w_pw1, conf_b_pw1=conf_b_pw1,
             conf_w_dw=conf_w_dw, conf_b_dw=conf_b_dw, conf_ln_g=conf_ln_g, conf_ln_b=conf_ln_b,
             conf_w_pw2=conf_w_pw2, conf_b_pw2=conf_b_pw2, swa_w_qkv=swa_w_qkv, swa_w_o=swa_w_o,
             swa_sinks=swa_sinks, gdn_w_in=gdn_w_in, gdn_w_conv=gdn_w_conv,
             gdn_a_log=gdn_a_log, gdn_dt_bias=gdn_dt_bias, gdn_norm=gdn_norm, gdn_w_o=gdn_w_o,
             sconv_w_in=sconv_w_in, sconv_w_conv=sconv_w_conv, sconv_w_out=sconv_w_out)
    bsz, seq, d = x_prompt.shape
    bn = x_sample.shape[0]
    n_c = bsz + bn
    pad = (-n_c) % ADA_ROWS_PAD
    c_all = jnp.concatenate([c_prompt, c_sample, jnp.zeros((pad, d), F32)], axis=0)
    mods = ada_all(c_all, w_ada, b_ada)
    st = dict(conf=state_conf_conv, k=cache_swa_k, v=cache_swa_v, ssm=state_gdn_ssm,
              gconv=state_gdn_conv, sconv=state_sconv)
    yp, sp = _trunk(x_prompt.reshape(bsz * seq, d),
                    lambda i: _layer_mods(mods, i, 0, bsz, True), True, bsz, seq, None, P)
    ys, ss = _trunk(x_sample.reshape(bn, d),
                    lambda i: _layer_mods(mods, i, bsz, bsz + bn, False), False, bn, 1, st, P)
    names = ('conf', 'k', 'v', 'ssm', 'gconv', 'sconv')
    outs = [yp.reshape(bsz, seq, d), ys.reshape(bn, 1, d)]
    for nm in names:
        outs += [sp[nm][None], ss[nm][None]]
    return tuple(outs)
```

```python
import functools
import math

import jax
import jax.numpy as jnp
from jax import lax
from jax.experimental import pallas as pl
from jax.experimental.pallas import tpu as pltpu

F32 = jnp.float32
BF16 = jnp.bfloat16

D_MODEL = 1024
DEPTH = 4
D_FF = 4 * D_MODEL
CONF_K = 31
HEAD_DIM = 64
N_HEADS = 16
N_KV = 4
WINDOW = 128
ROT_DIM = 16
ROPE_THETA = 500000.0
PAST_LEN = 8192
GDN_DK = 128
GDN_DV = 128
GDN_HK = 8
GDN_HV = 16
GDN_QK_W = GDN_HK * GDN_DK
GDN_V_W = GDN_HV * GDN_DV
GDN_CONV_DIM = 2 * GDN_QK_W + GDN_V_W
GDN_CONV_K = 4
GDN_CHUNK = 64
SCONV_K = 3
RMS_EPS = 1e-6
LN_EPS = 1e-5
L2_EPS = 1e-6

LANES = 128
SUBLANES = 8
HALO_MIN = 16
VMEM_LIMIT = 56 * 1024 * 1024


def _cparams(sem):
    return pltpu.CompilerParams(dimension_semantics=sem, vmem_limit_bytes=VMEM_LIMIT)


def _rows2d(ref):
    return ref[0] if len(ref.shape) == 3 else ref[...]


def _mod_spec(arr, tm, rows_per_seq):
    d = arr.shape[-1]
    if arr.ndim == 3:
        return pl.BlockSpec((1, 1, d), lambda i, *_: ((i * tm) // rows_per_seq, 0, 0))
    return pl.BlockSpec((tm, d), lambda i, *_: (i, 0))


def _const_spec(arr):
    nd = arr.ndim
    return pl.BlockSpec(arr.shape, lambda *_: (0,) * nd)


def _resident_spec(arr):
    nd = arr.ndim
    return pl.BlockSpec(arr.shape, lambda *_: (0,) * nd, pipeline_mode=pl.Buffered(1))


def _normmod(x, g, sh, sc):
    y = x * lax.rsqrt(jnp.mean(x * x, axis=-1, keepdims=True) + RMS_EPS)
    y = y * g
    return y * (1.0 + sc) + sh


def _sigmoid(x):
    return jax.nn.sigmoid(x)


def _silu(x):
    return x * jax.nn.sigmoid(x)


def _dot(a, b):
    return jnp.dot(a.astype(BF16), b.astype(BF16), preferred_element_type=F32)


def _ada_kernel(c_ref, w_ref, b_ref, o_ref):
    cm = _silu(c_ref[...])
    o_ref[0] = _dot(cm, w_ref[0]) + b_ref[0]


def ada_all(c, w_ada, b_ada, *, tn=1024):
    depth, d, n = w_ada.shape
    r = c.shape[0]
    return pl.pallas_call(
        _ada_kernel,
        out_shape=jax.ShapeDtypeStruct((depth, r, n), F32),
        grid=(depth, n // tn),
        in_specs=[pl.BlockSpec((r, d), lambda l, j: (0, 0)),
                  pl.BlockSpec((1, d, tn), lambda l, j: (l, 0, j)),
                  pl.BlockSpec((1, 1, tn), lambda l, j: (l, 0, j))],
        out_specs=pl.BlockSpec((1, r, tn), lambda l, j: (l, 0, j)),
        compiler_params=_cparams(("parallel", "parallel")),
        name="ada_all",
    )(c, w_ada, b_ada.reshape(depth, 1, n))


def _nmm_kernel(*refs, splits, has_bias, n_extra, n_out, epilogue, tn, side_fn, n_side_in,
                n_side_out):
    x_ref, g_ref, sh_ref, sc_ref, w_ref = refs[:5]
    p = 5
    b_ref = None
    if has_bias:
        b_ref = refs[p]; p += 1
    e_refs = refs[p:p + n_extra]; p += n_extra
    si_refs = refs[p:p + n_side_in]; p += n_side_in
    o_refs = refs[p:p + n_out]; p += n_out
    so_refs = refs[p:p + n_side_out]
    h = _normmod(x_ref[...], g_ref[...], _rows2d(sh_ref), _rows2d(sc_ref)).astype(BF16)
    if side_fn is not None:
        side_fn(h, si_refs, so_refs)
    extras = [r[...] for r in e_refs]
    ng = w_ref.shape[1] // splits
    for j in range(ng // tn):
        accs = []
        for s in range(splits):
            c0 = s * ng + j * tn
            a = jnp.dot(h, w_ref[:, c0:c0 + tn], preferred_element_type=F32)
            if has_bias:
                a = a + b_ref[:, c0:c0 + tn]
            accs.append(a)
        outs = epilogue(accs, extras, j * tn)
        for o_ref, o in zip(o_refs, outs):
            o_ref[:, j * tn:(j + 1) * tn] = o.astype(o_ref.dtype)


def nmm(x, gnorm, sh, sc, w, bias, *, splits, n_out, epilogue, tm, tn, rows_per_seq,
        out_dtype, extras=(), extra_specs=(), side=None, name):
    m, d = x.shape
    side_fn, side_in, side_in_specs, side_out_shapes, side_out_specs = (
        side if side is not None else (None, (), (), (), ()))
    n = w.shape[1]
    ng = n // splits
    in_specs = [pl.BlockSpec((tm, d), lambda i: (i, 0)),
                pl.BlockSpec((1, d), lambda i: (0, 0)),
                _mod_spec(sh, tm, rows_per_seq), _mod_spec(sc, tm, rows_per_seq),
                _resident_spec(w)]
    args = [x, gnorm.reshape(1, d), sh, sc, w]
    if bias is not None:
        b2 = bias.reshape(1, n)
        in_specs.append(_const_spec(b2))
        args.append(b2)
    in_specs += list(extra_specs) + list(side_in_specs)
    args += list(extras) + list(side_in)
    kern = functools.partial(_nmm_kernel, splits=splits, has_bias=bias is not None,
                             n_extra=len(extras), n_out=n_out, epilogue=epilogue, tn=tn,
                             side_fn=side_fn, n_side_in=len(side_in),
                             n_side_out=len(side_out_shapes))
    outs = pl.pallas_call(
        kern,
        out_shape=[jax.ShapeDtypeStruct((m, ng), out_dtype)] * n_out + list(side_out_shapes),
        grid=(m // tm,),
        in_specs=in_specs,
        out_specs=[pl.BlockSpec((tm, ng), lambda i: (i, 0))] * n_out + list(side_out_specs),
        compiler_params=_cparams(("parallel",)),
        name=name,
    )(*args)
    return outs


def _ep_plain(accs, extras, col0):
    return [accs[0]]


def _ep_glu(accs, extras, col0):
    return [accs[0] * _sigmoid(accs[1])]


def _ep_sconv(accs, extras, col0):
    return [accs[0], accs[1] * accs[2]]


MLP_TF = 512


def _mlp_compute(x, mlp_refs, o_ref):
    g_ref, sh_ref, sc_ref, gate_ref, wu_ref, wd_ref = mlp_refs[:6]
    h = _normmod(x, g_ref[...], _rows2d(sh_ref), _rows2d(sc_ref)).astype(BF16)
    acc = jnp.zeros(x.shape, F32)
    for c in range(wu_ref.shape[1] // MLP_TF):
        cols = slice(c * MLP_TF, (c + 1) * MLP_TF)
        a = jnp.dot(h, wu_ref[:, cols], preferred_element_type=F32)
        a = jnp.square(jnp.maximum(a, 0.0)).astype(BF16)
        acc = acc + jnp.dot(a, wd_ref[cols, :], preferred_element_type=F32)
    y = x + _rows2d(gate_ref) * acc
    if len(mlp_refs) > 6:
        y = y * lax.rsqrt(jnp.mean(y * y, axis=-1, keepdims=True) + RMS_EPS) * mlp_refs[6][...]
    o_ref[...] = y


def _mlp_operands(mlp, tm, rows_per_seq):
    gnorm, sh, sc, gate, w_up, w_down, norm_final = mlp
    d = w_up.shape[0]
    vec = pl.BlockSpec((1, d), lambda i: (0, 0))
    specs = [vec, _mod_spec(sh, tm, rows_per_seq), _mod_spec(sc, tm, rows_per_seq),
             _mod_spec(gate, tm, rows_per_seq), _resident_spec(w_up), _resident_spec(w_down)]
    args = [gnorm.reshape(1, d), sh, sc, gate, w_up, w_down]
    if norm_final is not None:
        specs.append(vec)
        args.append(norm_final.reshape(1, d))
    return specs, args


def _pro_id(mains, pars):
    return mains[0]


def _pro_conf(mains, pars):
    bdw, lng, lnb = pars
    y = mains[0] + bdw
    mu = jnp.mean(y, axis=-1, keepdims=True)
    var = jnp.mean(jnp.square(y - mu), axis=-1, keepdims=True)
    yn = (y - mu) * lax.rsqrt(var + LN_EPS) * lng + lnb
    return _silu(yn)


def _pro_mul(mains, pars):
    return mains[1].astype(F32) * mains[0]


def _pro_gdn(mains, pars):
    o, z = [a.astype(F32) for a in mains]
    ng = pars[0]
    outs = []
    for hh in range(GDN_HV):
        oh = o[:, hh * GDN_DV:(hh + 1) * GDN_DV]
        yh = oh * lax.rsqrt(jnp.mean(oh * oh, axis=-1, keepdims=True) + RMS_EPS) * ng
        outs.append(yh * _silu(z[:, hh * GDN_DV:(hh + 1) * GDN_DV]))
    return jnp.concatenate(outs, axis=-1)


def _tail_compute(a, w_ref, b_ref, x_ref, gate_ref, mlp_refs, o_ref):
    out = jnp.dot(a.astype(BF16), w_ref[...], preferred_element_type=F32)
    if b_ref is not None:
        out = out + b_ref[...]
    _mlp_compute(x_ref[...] + _rows2d(gate_ref) * out, mlp_refs, o_ref)


def _tail_kernel(*refs, n_main, n_par, has_bias, prologue):
    mains = [r[...] for r in refs[:n_main]]
    p = n_main
    pars = [r[...] for r in refs[p:p + n_par]]; p += n_par
    w_ref = refs[p]; p += 1
    b_ref = None
    if has_bias:
        b_ref = refs[p]; p += 1
    x_ref, gate_ref = refs[p:p + 2]
    _tail_compute(prologue(mains, pars), w_ref, b_ref, x_ref, gate_ref, refs[p + 2:-1], refs[-1])


def _main_spec(a, tm):
    if isinstance(a, tuple):
        arr, width, cb = a
        return arr, pl.BlockSpec((tm, width), lambda i, cb=cb: (i, cb))
    return a, pl.BlockSpec((tm, a.shape[1]), lambda i: (i, 0))


def tail(mains, pars, w, bias, x, gate, mlp, *, prologue, tm, rows_per_seq, name):
    m, d = x.shape
    mlp_specs, mlp_args = _mlp_operands(mlp, tm, rows_per_seq)
    in_specs, args = [], []
    for a in mains:
        arr, spec = _main_spec(a, tm)
        in_specs.append(spec)
        args.append(arr)
    for prm in pars:
        prm = prm.reshape(1, -1)
        in_specs.append(_const_spec(prm))
        args.append(prm)
    in_specs.append(_resident_spec(w)); args.append(w)
    if bias is not None:
        b2 = bias.reshape(1, -1)
        in_specs.append(_const_spec(b2)); args.append(b2)
    in_specs += [pl.BlockSpec((tm, d), lambda i: (i, 0)), _mod_spec(gate, tm, rows_per_seq)]
    in_specs += mlp_specs
    args += [x, gate] + mlp_args
    return pl.pallas_call(
        functools.partial(_tail_kernel, n_main=len(mains), n_par=len(pars),
                          has_bias=bias is not None, prologue=prologue),
        out_shape=jax.ShapeDtypeStruct((m, d), F32),
        grid=(m // tm,),
        in_specs=in_specs,
        out_specs=pl.BlockSpec((tm, d), lambda i: (i, 0)),
        compiler_params=_cparams(("parallel",)),
        name=name,
    )(*args)


def _dwconv_tile(win_ref, w_ref, k_taps, base, r0, rows, l0, lc):
    acc = None
    for k in range(k_taps):
        t = w_ref[k:k + 1, l0:l0 + lc] * win_ref[r0 + base + k:r0 + base + k + rows, l0:l0 + lc]
        acc = t if acc is None else acc + t
    return acc


def _fill_window(win, cur_ref, halo_ref, halo, tiles_per_seq):
    first = (pl.program_id(0) % tiles_per_seq) == 0
    win[0:halo, :] = jnp.where(first, 0.0, halo_ref[...].astype(F32))
    win[halo:, :] = cur_ref[...].astype(F32)


def _conv_tail_kernel(*refs, k_taps, halo, tt, tiles_per_seq, rc, lc, n_main, n_par,
                      has_bias, prologue):
    u_ref, halo_ref, wc_ref = refs[:3]
    p = 3
    main_refs = refs[p:p + n_main]; p += n_main
    par_refs = refs[p:p + n_par]; p += n_par
    w_ref = refs[p]; p += 1
    b_ref = None
    if has_bias:
        b_ref = refs[p]; p += 1
    x_ref, gate_ref = refs[p:p + 2]
    mlp_refs = refs[p + 2:-3]
    o_ref, win, ybuf = refs[-3:]
    _fill_window(win, u_ref, halo_ref, halo, tiles_per_seq)
    c = u_ref.shape[1]
    base = halo - (k_taps - 1)
    for r in range(tt // rc):
        for l in range(c // lc):
            ybuf[r * rc:(r + 1) * rc, l * lc:(l + 1) * lc] = _dwconv_tile(
                win, wc_ref, k_taps, base, r * rc, rc, l * lc, lc)
    mains = [ybuf[...]] + [r[...] for r in main_refs]
    pars = [r[...] for r in par_refs]
    _tail_compute(prologue(mains, pars), w_ref, b_ref, x_ref, gate_ref, mlp_refs, o_ref)


SEG = 32
PITCH = 36


def _conv_tail_strided_kernel(*refs, k_taps, tt, tiles_per_seq, n_main, n_par, has_bias,
                              prologue):
    u_ref, halo_ref, wc_ref = refs[:3]
    p = 3
    main_refs = refs[p:p + n_main]; p += n_main
    par_refs = refs[p:p + n_par]; p += n_par
    w_ref = refs[p]; p += 1
    b_ref = None
    if has_bias:
        b_ref = refs[p]; p += 1
    x_ref, gate_ref = refs[p:p + 2]
    mlp_refs = refs[p + 2:-3]
    o_ref, winp, yp = refs[-3:]
    c = u_ref.shape[1]
    nslab = c // LANES
    nvr = tt // SUBLANES
    first = (pl.program_id(0) % tiles_per_seq) == 0
    for l in range(nslab):
        lanes = slice(l * LANES, (l + 1) * LANES)
        winp[l, 0:SEG, :] = jnp.where(first, 0.0, halo_ref[:, lanes].astype(F32))
        for sg in range(tt // SEG):
            winp[l, PITCH * (sg + 1):PITCH * (sg + 1) + SEG, :] = (
                u_ref[sg * SEG:(sg + 1) * SEG, lanes].astype(F32))
    for l in range(nslab):
        lanes = slice(l * LANES, (l + 1) * LANES)
        taps = [jnp.broadcast_to(wc_ref[k:k + 1, lanes], (SUBLANES, LANES))
                for k in range(k_taps)]
        for j in range(nvr):
            acc = None
            for k in range(k_taps):
                i0 = SEG + j + k - (k_taps - 1)
                start = i0 + (PITCH - SEG) * (i0 // SEG)
                t = taps[k] * winp[l, pl.ds(start, SUBLANES, stride=PITCH), :]
                acc = t if acc is None else acc + t
            yp[l, j * SUBLANES:(j + 1) * SUBLANES, :] = acc
    cols = []
    for l in range(nslab):
        cols.append(jnp.concatenate(
            [yp[l, pl.ds((SUBLANES * a % nvr) * SUBLANES + SUBLANES * a // nvr, SUBLANES,
                         stride=SUBLANES), :] for a in range(nvr)], axis=0))
    mains = [jnp.concatenate(cols, axis=1)] + [r[...] for r in main_refs]
    pars = [r[...] for r in par_refs]
    _tail_compute(prologue(mains, pars), w_ref, b_ref, x_ref, gate_ref, mlp_refs, o_ref)


def conv_tail(u, wconv, mains, pars, w, bias, x, gate, mlp, *, halo, tt, rows_per_seq,
              prologue, name, rc=32, lc=256, strided=False):
    m, d = x.shape
    mlp_specs, mlp_args = _mlp_operands(mlp, tt, rows_per_seq)
    c = u.shape[1]
    k_taps = wconv.shape[0]
    hb = tt // halo
    in_specs = [pl.BlockSpec((tt, c), lambda i: (i, 0)),
                pl.BlockSpec((halo, c), lambda i: (jnp.maximum(i * hb - 1, 0), 0)),
                _const_spec(wconv)]
    args = [u, u, wconv]
    for a in mains:
        arr, spec = _main_spec(a, tt)
        in_specs.append(spec)
        args.append(arr)
    for prm in pars:
        prm = prm.reshape(1, -1)
        in_specs.append(_const_spec(prm)); args.append(prm)
    in_specs.append(_resident_spec(w)); args.append(w)
    if bias is not None:
        b2 = bias.reshape(1, -1)
        in_specs.append(_const_spec(b2)); args.append(b2)
    in_specs += [pl.BlockSpec((tt, d), lambda i: (i, 0)), _mod_spec(gate, tt, rows_per_seq)]
    in_specs += mlp_specs
    args += [x, gate] + mlp_args
    if strided:
        assert halo == SEG and tt == SEG * SUBLANES and k_taps - 1 <= SEG
        kern = functools.partial(
            _conv_tail_strided_kernel, k_taps=k_taps, tt=tt, tiles_per_seq=rows_per_seq // tt,
            n_main=len(mains), n_par=len(pars), has_bias=bias is not None, prologue=prologue)
        scratch = [pltpu.VMEM((c // LANES, PITCH * (tt // SEG + 1), LANES), F32),
                   pltpu.VMEM((c // LANES, tt, LANES), F32)]
    else:
        kern = functools.partial(
            _conv_tail_kernel, k_taps=k_taps, halo=halo, tt=tt,
            tiles_per_seq=rows_per_seq // tt, rc=rc, lc=lc, n_main=len(mains), n_par=len(pars),
            has_bias=bias is not None, prologue=prologue)
        scratch = [pltpu.VMEM((halo + tt, c), F32), pltpu.VMEM((tt, c), F32)]
    return pl.pallas_call(
        kern,
        out_shape=jax.ShapeDtypeStruct((m, d), F32),
        grid=(m // tt,),
        in_specs=in_specs,
        out_specs=pl.BlockSpec((tt, d), lambda i: (i, 0)),
        scratch_shapes=scratch,
        compiler_params=_cparams(("parallel",)),
        name=name,
    )(*args)


def _conv_step_kernel(st_ref, u_ref, w_ref, y_ref, ns_ref, *, k_taps, c, lc, act):
    for l in range(c // lc):
        u = u_ref[:, l * lc:(l + 1) * lc]
        acc = w_ref[k_taps - 1:k_taps, l * lc:(l + 1) * lc] * u
        for k in range(k_taps - 1):
            s = st_ref[:, k * c + l * lc:k * c + (l + 1) * lc]
            acc = acc + w_ref[k:k + 1, l * lc:(l + 1) * lc] * s
            if k >= 1:
                ns_ref[:, (k - 1) * c + l * lc:(k - 1) * c + (l + 1) * lc] = s
        ns_ref[:, (k_taps - 2) * c + l * lc:(k_taps - 2) * c + (l + 1) * lc] = u
        y_ref[:, l * lc:(l + 1) * lc] = act(acc)


def conv_step(state, u, wconv, *, act=lambda v: v, bb=32, name):
    bn, km1, c = state.shape
    k_taps = km1 + 1
    bb = min(bb, bn)
    st2 = state.reshape(bn, km1 * c)
    y, ns = pl.pallas_call(
        functools.partial(_conv_step_kernel, k_taps=k_taps, c=c, lc=min(c, 1024), act=act),
        out_shape=[jax.ShapeDtypeStruct((bn, c), F32),
                   jax.ShapeDtypeStruct((bn, km1 * c), F32)],
        grid=(bn // bb,),
        in_specs=[pl.BlockSpec((bb, km1 * c), lambda i: (i, 0)),
                  pl.BlockSpec((bb, c), lambda i: (i, 0)),
                  _const_spec(wconv)],
        out_specs=[pl.BlockSpec((bb, c), lambda i: (i, 0)),
                   pl.BlockSpec((bb, km1 * c), lambda i: (i, 0))],
        compiler_params=_cparams(("parallel",)),
        name=name,
    )(st2, u, wconv)
    return y, ns.reshape(bn, km1, c)


QKV_TN = 512
N_ROPE_COLS = (N_HEADS + N_KV) * HEAD_DIM


def rope_tables(pos):
    half = ROT_DIM // 2
    inv = jnp.power(jnp.float32(ROPE_THETA), -jnp.arange(half, dtype=F32) * 2.0 / ROT_DIM)
    ang = pos.astype(F32)[:, None] * inv[None, :]
    cos, sin = jnp.cos(ang), jnp.sin(ang)
    lane = jnp.arange(LANES) % HEAD_DIM
    idx = lane % half
    cosf = jnp.where(lane < ROT_DIM, cos[:, idx], 1.0)
    sina = jnp.where(lane < half, -sin[:, idx], 0.0)
    sinb = jnp.where((lane >= half) & (lane < ROT_DIM), sin[:, idx], 0.0)
    return cosf, sina, sinb


def _ep_rope(accs, extras, col0):
    a = accs[0]
    cosf, sina, sinb = extras
    half = ROT_DIM // 2
    outs = []
    for g in range(a.shape[1] // LANES):
        xg = a[:, g * LANES:(g + 1) * LANES]
        if col0 + g * LANES < N_ROPE_COLS:
            xg = (xg * cosf + pltpu.roll(xg, LANES - half, 1) * sina
                  + pltpu.roll(xg, half, 1) * sinb)
        outs.append(xg)
    return [jnp.concatenate(outs, axis=-1)]


def _dot_nt(a, b):
    return lax.dot_general(a, b, (((1,), (1,)), ((), ())), preferred_element_type=F32)


def _dot_tn(a, b):
    return lax.dot_general(a, b, (((0,), (0,)), ((), ())), preferred_element_type=F32)


def _swa_kernel(sink_ref, q_ref, kc_ref, kp_ref, vc_ref, vp_ref, o_ref):
    n = pl.program_id(1)
    w = WINDOW
    lane = lax.broadcasted_iota(jnp.int32, (1, LANES), 1)
    lo = lane < HEAD_DIM
    row = lax.broadcasted_iota(jnp.int32, (2 * w, 1), 0)
    qi = row % w
    kj = lax.broadcasted_iota(jnp.int32, (1, 2 * w), 1)
    band = (kj >= qi) & (kj <= qi + w) & ((kj >= w) | (n > 0))
    top = row < w
    scale = HEAD_DIM ** -0.5
    for h in range(N_KV):
        cg = (h // 2) * LANES
        kg = jnp.concatenate([kp_ref[:, cg:cg + LANES], kc_ref[:, cg:cg + LANES]], axis=0)
        vg = jnp.concatenate([vp_ref[:, cg:cg + LANES], vc_ref[:, cg:cg + LANES]], axis=0)
        kr = pltpu.roll(kg, HEAD_DIM, 1)
        vr = pltpu.roll(vg, HEAD_DIM, 1)
        if h % 2 == 0:
            k_lo, k_hi = jnp.where(lo, kg, 0.0), jnp.where(lo, 0.0, kr)
            v_lo, v_hi = jnp.where(lo, vg, 0.0), jnp.where(lo, 0.0, vr)
        else:
            k_lo, k_hi = jnp.where(lo, kr, 0.0), jnp.where(lo, 0.0, kg)
            v_lo, v_hi = jnp.where(lo, vr, 0.0), jnp.where(lo, 0.0, vg)
        c0 = h * 4 * HEAD_DIM
        q2 = jnp.concatenate([q_ref[:, c0:c0 + LANES], q_ref[:, c0 + LANES:c0 + 2 * LANES]],
                             axis=0).astype(BF16)
        o2 = None
        for par, kpad, vpad in ((0, k_lo, v_lo), (1, k_hi, v_hi)):
            s = _dot_nt(q2, kpad.astype(BF16)) * scale
            s = jnp.where(band, s, -jnp.inf)
            sink = jnp.where(top, sink_ref[4 * h + par], sink_ref[4 * h + 2 + par])
            m = jnp.maximum(jnp.max(s, axis=-1, keepdims=True), sink)
            p = jnp.exp(s - m)
            den = jnp.sum(p, axis=-1, keepdims=True) + jnp.exp(sink - m)
            part = jnp.dot(p.astype(BF16), vpad.astype(BF16), preferred_element_type=F32) / den
            o2 = part if o2 is None else o2 + part
        o_ref[:, c0:c0 + LANES] = o2[:w].astype(o_ref.dtype)
        o_ref[:, c0 + LANES:c0 + 2 * LANES] = o2[w:].astype(o_ref.dtype)


def swa_prompt(qkv, sinks, *, bsz, seq):
    nb = seq // WINDOW
    dq = N_HEADS * HEAD_DIM
    dkv = N_KV * HEAD_DIM
    kcol, vcol = dq // dkv, dq // dkv + 1
    cur = lambda b, n: b * nb + n
    prev = lambda b, n: b * nb + jnp.maximum(n - 1, 0)
    return pl.pallas_call(
        _swa_kernel,
        out_shape=jax.ShapeDtypeStruct((bsz * seq, dq), BF16),
        grid=(bsz, nb),
        in_specs=[pl.BlockSpec(memory_space=pltpu.SMEM),
                  pl.BlockSpec((WINDOW, dq), lambda b, n: (cur(b, n), 0)),
                  pl.BlockSpec((WINDOW, dkv), lambda b, n: (cur(b, n), kcol)),
                  pl.BlockSpec((WINDOW, dkv), lambda b, n: (prev(b, n), kcol)),
                  pl.BlockSpec((WINDOW, dkv), lambda b, n: (cur(b, n), vcol)),
                  pl.BlockSpec((WINDOW, dkv), lambda b, n: (prev(b, n), vcol))],
        out_specs=pl.BlockSpec((WINDOW, dq), lambda b, n: (cur(b, n), 0)),
        compiler_params=_cparams(("parallel", "parallel")),
        name="swa_prompt",
    )(sinks, qkv, qkv, qkv, qkv, qkv)


def _swa_decode_kernel(sink_ref, q_ref, kn_ref, vn_ref, kc_ref, vc_ref, o_ref, ko_ref, vo_ref,
                       *, bb):
    w = WINDOW
    dkv = N_KV * HEAD_DIM
    lane_head = lax.broadcasted_iota(jnp.int32, (N_KV, dkv), 1) // HEAD_DIM
    row_head = lax.broadcasted_iota(jnp.int32, (N_KV, dkv), 0)
    own = lane_head == row_head
    sinks = sink_ref[...]
    scale = HEAD_DIM ** -0.5

    def body(b, carry):
        qb = q_ref[b]
        qblk = jnp.concatenate(
            [jnp.where(own, jnp.broadcast_to(qb[g:g + 1, :], (N_KV, dkv)), 0.0)
             for g in range(N_HEADS // N_KV)], axis=0)
        kc = kc_ref[b]
        vc = vc_ref[b]
        kn = kn_ref[b]
        vn = vn_ref[b]
        qb16 = qblk.astype(BF16)
        s_c = _dot_nt(qb16, kc.astype(BF16)) * scale
        s_n = jnp.sum(qblk * kn, axis=-1, keepdims=True) * scale
        m = jnp.maximum(jnp.maximum(jnp.max(s_c, axis=-1, keepdims=True), s_n), sinks)
        p_c = jnp.exp(s_c - m)
        p_n = jnp.exp(s_n - m)
        den = jnp.sum(p_c, axis=-1, keepdims=True) + p_n + jnp.exp(sinks - m)
        p_c = p_c / den
        p_n = p_n / den
        o = (jnp.dot(p_c.astype(BF16), vc.astype(BF16), preferred_element_type=F32)
             + p_n * vn)
        for g in range(N_HEADS // N_KV):
            og = jnp.sum(jnp.where(own, o[g * N_KV:(g + 1) * N_KV, :], 0.0), axis=0,
                         keepdims=True)
            o_ref[b, g:g + 1, :] = og
        ko_ref[b, 0:w - 1, :] = kc[1:w, :]
        ko_ref[b, w - 1:w, :] = kn
        vo_ref[b, 0:w - 1, :] = vc[1:w, :]
        vo_ref[b, w - 1:w, :] = vn
        return carry

    lax.fori_loop(0, bb, body, 0)


def swa_decode(qg, k_new, v_new, k_cache, v_cache, sinks_gh, *, bb=8):
    bn = qg.shape[0]
    dkv = N_KV * HEAD_DIM
    ng = N_HEADS // N_KV
    return pl.pallas_call(
        functools.partial(_swa_decode_kernel, bb=bb),
        out_shape=[jax.ShapeDtypeStruct((bn, ng, dkv), F32),
                   jax.ShapeDtypeStruct((bn, WINDOW, dkv), F32),
                   jax.ShapeDtypeStruct((bn, WINDOW, dkv), F32)],
        grid=(bn // bb,),
        in_specs=[_const_spec(sinks_gh),
                  pl.BlockSpec((bb, ng, dkv), lambda i: (i, 0, 0)),
                  pl.BlockSpec((bb, 1, dkv), lambda i: (i, 0, 0)),
                  pl.BlockSpec((bb, 1, dkv), lambda i: (i, 0, 0)),
                  pl.BlockSpec((bb, WINDOW, dkv), lambda i: (i, 0, 0)),
                  pl.BlockSpec((bb, WINDOW, dkv), lambda i: (i, 0, 0))],
        out_specs=[pl.BlockSpec((bb, ng, dkv), lambda i: (i, 0, 0)),
                   pl.BlockSpec((bb, WINDOW, dkv), lambda i: (i, 0, 0)),
                   pl.BlockSpec((bb, WINDOW, dkv), lambda i: (i, 0, 0))],
        compiler_params=_cparams(("parallel",)),
        name="swa_decode",
    )(sinks_gh, qg, k_new.reshape(bn, 1, dkv), v_new.reshape(bn, 1, dkv), k_cache, v_cache)


GDN_BA_PAD = LANES


def _gdn_side(h, si_refs, so_refs):
    wba_ref, wbat_ref = si_refs
    ba_ref, bat_ref = so_refs
    ba_ref[...] = jnp.dot(h, wba_ref[...], preferred_element_type=F32)
    bat_ref[...] = _dot_nt(wbat_ref[...], h)


def _gdn_gates(b, a, alog, dtb):
    x = a + dtb
    sp = jnp.maximum(x, 0.0) + jnp.log1p(jnp.exp(-jnp.abs(x)))
    return -jnp.exp(alog) * sp, _sigmoid(b)


def _l2n(x):
    return x * lax.rsqrt(jnp.sum(x * x, axis=-1, keepdims=True) + L2_EPS)


def _gdn_store_qkv(y, col, rows, q_ref, k_ref, v_ref):
    if col < GDN_QK_W:
        q_ref[rows, col:col + GDN_DK] = (_l2n(y) * (GDN_DK ** -0.5)).astype(q_ref.dtype)
    elif col < 2 * GDN_QK_W:
        k_ref[rows, col - GDN_QK_W:col - GDN_QK_W + GDN_DK] = _l2n(y).astype(k_ref.dtype)
    else:
        v_ref[rows, col - 2 * GDN_QK_W:col - 2 * GDN_QK_W + GDN_DV] = y.astype(v_ref.dtype)


def _gdn_prep_kernel(cur_ref, halo_ref, wc_ref, ba_ref, bat_ref, alr_ref, dtr_ref, alc_ref,
                     dtc_ref, q_ref, k_ref, v_ref, g_ref, b_ref, gt_ref, win, *, tt,
                     tiles_per_seq, rc):
    halo = HALO_MIN
    _fill_window(win, cur_ref, halo_ref, halo, tiles_per_seq)
    base = halo - (GDN_CONV_K - 1)
    for r in range(tt // rc):
        rows = slice(r * rc, (r + 1) * rc)
        for l in range(GDN_CONV_DIM // LANES):
            y = _silu(_dwconv_tile(win, wc_ref, GDN_CONV_K, base, r * rc, rc, l * LANES, LANES))
            _gdn_store_qkv(y, l * LANES, rows, q_ref, k_ref, v_ref)
    ba = ba_ref[...]
    g, beta = _gdn_gates(ba[:, 0:GDN_HV], ba[:, GDN_HV:2 * GDN_HV], alr_ref[...], dtr_ref[...])
    bat = bat_ref[...]
    gt, _ = _gdn_gates(bat[0:GDN_HV, :], bat[GDN_HV:2 * GDN_HV, :], alc_ref[...], dtc_ref[...])
    r = lax.broadcasted_iota(jnp.int32, (tt, tt), 0)
    c = lax.broadcasted_iota(jnp.int32, (tt, tt), 1)
    same = (r // GDN_CHUNK) == (c // GDN_CHUNK)
    g_ref[...] = _dot_hi((same & (r >= c)).astype(F32), g)
    b_ref[...] = beta
    gct = _dot_hi(gt, (same & (r <= c)).astype(F32))
    for ci in range(tt // GDN_CHUNK):
        gt_ref[ci] = gct[:, ci * GDN_CHUNK:(ci + 1) * GDN_CHUNK]


def gdn_prep_prompt(proj, ba, bat, wconv, a_log, dt_bias, *, tt, rows_per_seq, rc=128):
    m = proj.shape[0]
    c = GDN_CONV_DIM
    hb = tt // HALO_MIN
    alr, dtr = a_log.reshape(1, GDN_HV), dt_bias.reshape(1, GDN_HV)
    alc, dtc = a_log.reshape(GDN_HV, 1), dt_bias.reshape(GDN_HV, 1)
    return pl.pallas_call(
        functools.partial(_gdn_prep_kernel, tt=tt, tiles_per_seq=rows_per_seq // tt, rc=rc),
        out_shape=[jax.ShapeDtypeStruct((m, GDN_QK_W), BF16),
                   jax.ShapeDtypeStruct((m, GDN_QK_W), BF16),
                   jax.ShapeDtypeStruct((m, GDN_V_W), BF16),
                   jax.ShapeDtypeStruct((m, GDN_HV), F32),
                   jax.ShapeDtypeStruct((m, GDN_HV), F32),
                   jax.ShapeDtypeStruct((m // GDN_CHUNK, GDN_HV, GDN_CHUNK), F32)],
        grid=(m // tt,),
        in_specs=[pl.BlockSpec((tt, c), lambda i: (i, 0)),
                  pl.BlockSpec((HALO_MIN, c), lambda i: (jnp.maximum(i * hb - 1, 0), 0)),
                  _const_spec(wconv),
                  pl.BlockSpec((tt, GDN_BA_PAD), lambda i: (i, 0)),
                  pl.BlockSpec((2 * GDN_HV, tt), lambda i: (0, i)),
                  _const_spec(alr), _const_spec(dtr), _const_spec(alc), _const_spec(dtc)],
        out_specs=[pl.BlockSpec((tt, GDN_QK_W), lambda i: (i, 0)),
                   pl.BlockSpec((tt, GDN_QK_W), lambda i: (i, 0)),
                   pl.BlockSpec((tt, GDN_V_W), lambda i: (i, 0)),
                   pl.BlockSpec((tt, GDN_HV), lambda i: (i, 0)),
                   pl.BlockSpec((tt, GDN_HV), lambda i: (i, 0)),
                   pl.BlockSpec((tt // GDN_CHUNK, GDN_HV, GDN_CHUNK), lambda i: (i, 0, 0))],
        scratch_shapes=[pltpu.VMEM((HALO_MIN + tt, c), F32)],
        compiler_params=_cparams(("parallel",)),
        name="gdn_prep_prompt",
    )(proj, proj, wconv, ba, bat, alr, dtr, alc, dtc)


def _gdn_prep_step_kernel(y_ref, ba_ref, alr_ref, dtr_ref, q_ref, k_ref, v_ref, g_ref, b_ref):
    rows = slice(None)
    for l in range(GDN_CONV_DIM // LANES):
        _gdn_store_qkv(y_ref[:, l * LANES:(l + 1) * LANES], l * LANES, rows, q_ref, k_ref, v_ref)
    ba = ba_ref[...]
    g, beta = _gdn_gates(ba[:, 0:GDN_HV], ba[:, GDN_HV:2 * GDN_HV], alr_ref[...], dtr_ref[...])
    g_ref[...] = g
    b_ref[...] = beta


def gdn_prep_step(y, ba, a_log, dt_bias):
    bn = y.shape[0]
    return pl.pallas_call(
        _gdn_prep_step_kernel,
        out_shape=[jax.ShapeDtypeStruct((bn, GDN_QK_W), F32),
                   jax.ShapeDtypeStruct((bn, GDN_QK_W), F32),
                   jax.ShapeDtypeStruct((bn, GDN_V_W), F32),
                   jax.ShapeDtypeStruct((bn, GDN_HV), F32),
                   jax.ShapeDtypeStruct((bn, GDN_HV), F32)],
        compiler_params=pltpu.CompilerParams(vmem_limit_bytes=VMEM_LIMIT),
        name="gdn_prep_step",
    )(y, ba, a_log.reshape(1, GDN_HV), dt_bias.reshape(1, GDN_HV))


def _dot_hi(a, b):
    return jnp.dot(a, b, precision=lax.Precision.HIGHEST, preferred_element_type=F32)


GDN_BD_HEADS = 4


def _gdn_chunk_kernel(q_ref, k_ref, v_ref, gc_ref, b_ref, gct_ref, o_ref, sfin_ref, s_scr,
                      *, c, bpb):
    n = pl.program_id(1)

    @pl.when(n == 0)
    def _():
        s_scr[...] = jnp.zeros(s_scr.shape, F32)

    row = lax.broadcasted_iota(jnp.int32, (c, 2 * c), 0)
    lane = lax.broadcasted_iota(jnp.int32, (c, 2 * c), 1)
    colp = lane % c
    lo = lane < c
    lower2 = row >= colp
    strict2 = row > colp
    eye2 = (row == colp).astype(F32)
    rep = GDN_HV // GDN_HK
    ppg = GDN_BD_HEADS // rep
    gw = GDN_BD_HEADS * c
    bdmask16 = (lax.broadcasted_iota(jnp.int32, (gw, gw), 0) // c
                == lax.broadcasted_iota(jnp.int32, (gw, gw), 1) // c).astype(BF16)
    eyecat = jnp.concatenate([eye2] * ppg, axis=1)
    rowc = lax.broadcasted_iota(jnp.int32, (c, gw), 0)
    colc = lax.broadcasted_iota(jnp.int32, (c, gw), 1) % c
    zpad = jnp.zeros((c, GDN_DV), BF16)

    def bd(x):
        return jnp.concatenate([x.astype(BF16)] * GDN_BD_HEADS, axis=0) * bdmask16

    def pair_stack(a0, a1):
        return jnp.concatenate([jnp.concatenate([a0, zpad], axis=1),
                                jnp.concatenate([zpad, a1], axis=1)], axis=0)

    gates = []
    for bi in range(bpb):
        gc = gc_ref[bi]
        glast = gc[c - 1:c, :]
        gates.append((gc, b_ref[bi], gct_ref[bi, 0], jnp.exp(gc), jnp.exp(glast - gc),
                      jnp.exp(glast)))

    pairs = [(bi, hk) for bi in range(bpb) for hk in range(GDN_HK)]
    heads = [(bi, hv) for bi in range(bpb) for hv in range(GDN_HV)]

    kf, qf, a_all = {}, {}, {}
    for bi, hk in pairs:
        kh = k_ref[bi, :, hk * GDN_DK:(hk + 1) * GDN_DK]
        qh = q_ref[bi, :, hk * GDN_DK:(hk + 1) * GDN_DK]
        kf[bi, hk], qf[bi, hk] = kh.astype(F32), qh.astype(F32)
        a_all[bi, hk] = _dot_nt(jnp.concatenate([kh, qh], axis=0),
                                jnp.concatenate([kh, kh], axis=0))

    ks, qs, s_old = {}, {}, {}
    for bi, hv in heads:
        gc, beta, _, eg, _, _ = gates[bi]
        hk = hv // rep
        egc = eg[:, hv:hv + 1]
        kq = jnp.concatenate([kf[bi, hk] * (beta[:, hv:hv + 1] * egc), qf[bi, hk] * egc], axis=0)
        s_old[bi, hv] = s_scr[bi * GDN_HV + hv]
        kqs = jnp.dot(kq.astype(BF16), s_old[bi, hv].astype(BF16), preferred_element_type=F32)
        ks[bi, hv], qs[bi, hv] = kqs[:c], kqs[c:]

    l2, attn2 = {}, {}
    for bi, hk in pairs:
        gc, beta, gct, _, _, _ = gates[bi]
        hv0, hv1 = hk * rep, hk * rep + 1
        gcol2 = jnp.where(lo, gc[:, hv0:hv0 + 1], gc[:, hv1:hv1 + 1])
        grow2 = jnp.concatenate([gct[hv0:hv0 + 1, :], gct[hv1:hv1 + 1, :]], axis=1)
        decay2 = jnp.exp(jnp.where(lower2, gcol2 - grow2, -jnp.inf))
        bcol2 = jnp.where(lo, beta[:, hv0:hv0 + 1], beta[:, hv1:hv1 + 1])
        l2[bi, hk] = jnp.where(strict2, bcol2 * a_all[bi, hk][:c] * decay2, 0.0)
        attn2[bi, hk] = (a_all[bi, hk][c:] * decay2).astype(BF16)

    groups = [(bi, g) for bi in range(bpb) for g in range(GDN_HK // ppg)]
    lcat = {(bi, g): jnp.concatenate([l2[bi, g * ppg + pi] for pi in range(ppg)], axis=1)
            for bi, g in groups}

    def e_mask(s):
        return (rowc // (2 * s) == colc // (2 * s)) & (rowc // s != colc // s)

    m1 = e_mask(1)
    t = {key: eyecat - jnp.where(m1, lcat[key], 0.0) for key in groups}
    s = 2
    while s < c:
        ms = e_mask(s)
        x = {key: jnp.dot(jnp.where(ms, lcat[key], 0.0).astype(BF16), bd(t[key]),
                          preferred_element_type=F32) for key in groups}
        t = {key: t[key] - jnp.dot(t[key].astype(BF16), bd(x[key]), preferred_element_type=F32)
             for key in groups}
        s *= 2

    vn = {}
    for bi, hk in pairs:
        _, beta, _, _, _, _ = gates[bi]
        ys = []
        for hv in (hk * rep, hk * rep + 1):
            vh = v_ref[bi, :, hv * GDN_DV:(hv + 1) * GDN_DV].astype(F32)
            ys.append((vh * beta[:, hv:hv + 1] - ks[bi, hv]).astype(BF16))
        pi = hk % ppg
        tp = t[bi, hk // ppg][:, pi * 2 * c:(pi + 1) * 2 * c].astype(BF16)
        vn[bi, hk] = jnp.dot(tp, pair_stack(ys[0], ys[1]),
                             preferred_element_type=F32).astype(BF16)

    for bi, hk in pairs:
        hv0 = hk * rep
        v2 = vn[bi, hk]
        o2 = jnp.dot(attn2[bi, hk], pair_stack(v2[:, :GDN_DV], v2[:, GDN_DV:]),
                     preferred_element_type=F32)
        o_ref[bi, :, hv0 * GDN_DV:(hv0 + rep) * GDN_DV] = (o2 + jnp.concatenate(
            [qs[bi, hv0], qs[bi, hv0 + 1]], axis=1)).astype(o_ref.dtype)

    for bi, hv in heads:
        _, _, _, _, ek, egl = gates[bi]
        hk = hv // rep
        kd = (kf[bi, hk] * ek[:, hv:hv + 1]).astype(BF16)
        v16 = vn[bi, hk][:, (hv % rep) * GDN_DV:(hv % rep + 1) * GDN_DV]
        s_scr[bi * GDN_HV + hv] = s_old[bi, hv] * egl[:, hv:hv + 1] + _dot_tn(kd, v16)

    @pl.when(n == pl.num_programs(1) - 1)
    def _():
        for bi in range(bpb):
            sfin_ref[bi] = s_scr[bi * GDN_HV:(bi + 1) * GDN_HV]


def gdn_chunked(q, k, v, gc, beta, gct, *, bsz, seq, c=GDN_CHUNK, bpb=2):
    nc = seq // c
    r3 = lambda a: a.reshape(bsz, seq, a.shape[-1])
    idx = lambda b, n: (b, n, 0)
    o, s_fin = pl.pallas_call(
        functools.partial(_gdn_chunk_kernel, c=c, bpb=bpb),
        out_shape=[jax.ShapeDtypeStruct((bsz, seq, GDN_V_W), BF16),
                   jax.ShapeDtypeStruct((bsz, GDN_HV, GDN_DK, GDN_DV), F32)],
        grid=(bsz // bpb, nc),
        in_specs=[pl.BlockSpec((bpb, c, GDN_QK_W), idx), pl.BlockSpec((bpb, c, GDN_QK_W), idx),
                  pl.BlockSpec((bpb, c, GDN_V_W), idx), pl.BlockSpec((bpb, c, GDN_HV), idx),
                  pl.BlockSpec((bpb, c, GDN_HV), idx),
                  pl.BlockSpec((bpb, 1, GDN_HV, c), lambda b, n: (b, n, 0, 0))],
        out_specs=[pl.BlockSpec((bpb, c, GDN_V_W), idx),
                   pl.BlockSpec((bpb, GDN_HV, GDN_DK, GDN_DV), lambda b, n: (b, 0, 0, 0))],
        scratch_shapes=[pltpu.VMEM((bpb * GDN_HV, GDN_DK, GDN_DV), F32)],
        compiler_params=_cparams(("parallel", "arbitrary")),
        name="gdn_chunked",
    )(r3(q), r3(k), r3(v), r3(gc), r3(beta), gct.reshape(bsz, nc, GDN_HV, c))
    return o.reshape(bsz * seq, GDN_V_W), s_fin


def _gdn_decode_kernel(q_ref, k_ref, v_ref, g_ref, b_ref, s_ref, o_ref, so_ref, qt_scr, kt_scr):
    b = pl.program_id(0)

    @pl.when(b == 0)
    def _():
        for hk in range(GDN_HK):
            qt_scr[hk] = q_ref[:, hk * GDN_DK:(hk + 1) * GDN_DK].T
            kt_scr[hk] = k_ref[:, hk * GDN_DK:(hk + 1) * GDN_DK].T

    pick = lax.broadcasted_iota(jnp.int32, (1, q_ref.shape[0]), 1) == b
    grow = g_ref[0]
    brow = b_ref[0]
    rep = GDN_HV // GDN_HK
    for hv in range(GDN_HV):
        hk = hv // rep
        kcol = jnp.sum(jnp.where(pick, kt_scr[hk], 0.0), axis=1, keepdims=True)
        qcol = jnp.sum(jnp.where(pick, qt_scr[hk], 0.0), axis=1, keepdims=True)
        s1 = s_ref[0, hv] * jnp.exp(grow[:, hv:hv + 1])
        kv = jnp.sum(kcol * s1, axis=0, keepdims=True)
        vrow = v_ref[0, :, hv * GDN_DV:(hv + 1) * GDN_DV]
        s2 = s1 + kcol * ((vrow - kv) * brow[:, hv:hv + 1])
        so_ref[0, hv] = s2
        o_ref[0, :, hv * GDN_DV:(hv + 1) * GDN_DV] = jnp.sum(qcol * s2, axis=0, keepdims=True)


def gdn_decode(q, k, v, g, beta, state):
    bn = q.shape[0]
    sspec = pl.BlockSpec((1, GDN_HV, GDN_DK, GDN_DV), lambda b: (b, 0, 0, 0))
    rowspec = lambda width: pl.BlockSpec((1, 1, width), lambda b: (b, 0, 0))
    o, s_new = pl.pallas_call(
        _gdn_decode_kernel,
        out_shape=[jax.ShapeDtypeStruct((bn, 1, GDN_V_W), F32),
                   jax.ShapeDtypeStruct(state.shape, F32)],
        grid=(bn,),
        in_specs=[_const_spec(q), _const_spec(k), rowspec(GDN_V_W), rowspec(GDN_HV),
                  rowspec(GDN_HV), sspec],
        out_specs=[rowspec(GDN_V_W), sspec],
        scratch_shapes=[pltpu.VMEM((GDN_HK, GDN_DK, bn), F32),
                        pltpu.VMEM((GDN_HK, GDN_DK, bn), F32)],
        compiler_params=_cparams(("arbitrary",)),
        name="gdn_decode",
    )(q, k, v.reshape(bn, 1, GDN_V_W), g.reshape(bn, 1, GDN_HV), beta.reshape(bn, 1, GDN_HV),
      state)
    return o.reshape(bn, GDN_V_W), s_new


N_MIXERS = 4
TM_PROMPT = 512
TT_PROMPT = 256
ADA_ROWS_PAD = SUBLANES


def _layer_mods(mods, i, lo, hi, per_seq):
    d = D_MODEL
    out = []
    for k in range(6):
        v = mods[i, lo:hi, k * d:(k + 1) * d]
        out.append(v.reshape(hi - lo, 1, d) if per_seq else v)
    return out


def _gdn_weights(w_in):
    n_main = GDN_CONV_DIM + GDN_V_W
    w_ba = w_in[:, n_main:n_main + 2 * GDN_HV]
    w_ba_pad = jnp.pad(w_ba, ((0, 0), (0, GDN_BA_PAD - 2 * GDN_HV)))
    return w_in[:, :n_main].astype(BF16), w_ba_pad.astype(BF16), w_ba.T.astype(BF16)


def _gdn_inproj(x, gnorm, sh, sc, w_main, w_ba, w_bat, *, tm, rows_per_seq, out_dtype):
    m = x.shape[0]
    side = (_gdn_side, (w_ba, w_bat), (_const_spec(w_ba), _const_spec(w_bat)),
            (jax.ShapeDtypeStruct((m, GDN_BA_PAD), F32),
             jax.ShapeDtypeStruct((2 * GDN_HV, m), F32)),
            (pl.BlockSpec((tm, GDN_BA_PAD), lambda i: (i, 0)),
             pl.BlockSpec((2 * GDN_HV, tm), lambda i: (0, i))))
    return nmm(x, gnorm, sh, sc, w_main, None, splits=1, n_out=1, epilogue=_ep_plain, tm=tm,
               tn=1024, rows_per_seq=rows_per_seq, out_dtype=out_dtype, side=side,
               name="gdn_inproj")


def _trunk(x, mods_of, prompt, bsz, seq, st, P):
    d = D_MODEL
    tm = TM_PROMPT if prompt else x.shape[0]
    rps = seq if prompt else 1
    act = BF16 if prompt else F32
    new = {}
    for i in range(DEPTH):
        mixer, j = i % N_MIXERS, i // N_MIXERS
        sh1, sc1, g1, sh2, sc2, g2 = mods_of(i)
        gn = P['norm_mix'][i]
        mlp = (P['norm_mlp'][i], sh2, sc2, g2, P['w_up'][i].astype(BF16),
               P['w_down'][i].astype(BF16), P['norm_final'] if i == DEPTH - 1 else None)
        if mixer == 0:
            u, = nmm(x, gn, sh1, sc1, P['conf_w_pw1'][j].astype(BF16), P['conf_b_pw1'][j],
                     splits=2, n_out=1, epilogue=_ep_glu, tm=tm, tn=512, rows_per_seq=rps,
                     out_dtype=act, name="conf_pw1")
            pars = [P['conf_b_dw'][j], P['conf_ln_g'][j], P['conf_ln_b'][j]]
            w2, b2 = P['conf_w_pw2'][j].astype(BF16), P['conf_b_pw2'][j]
            if prompt:
                x = conv_tail(u, P['conf_w_dw'][j], [], pars, w2, b2, x, g1, mlp, halo=32,
                              tt=TT_PROMPT, rows_per_seq=rps, prologue=_pro_conf,
                              name="conf_conv", strided=True)
                new['conf'] = u.reshape(bsz, seq, d)[:, seq - (CONF_K - 1):].astype(F32)
            else:
                y, new['conf'] = conv_step(st['conf'][j], u, P['conf_w_dw'][j], name="conf_step")
                x = tail([y], pars, w2, b2, x, g1, mlp, prologue=_pro_conf, tm=tm, rows_per_seq=rps,
                         name="conf_tail")
        elif mixer == 1:
            if prompt:
                tabs = rope_tables(jnp.arange(seq, dtype=jnp.int32))
                tspec = pl.BlockSpec((tm, LANES), lambda r: (r % (seq // tm), 0))
            else:
                pos = jnp.full((tm,), PAST_LEN, dtype=jnp.int32)
                tabs = rope_tables(pos)
                tspec = pl.BlockSpec((tm, LANES), lambda r: (0, 0))
            qkv, = nmm(x, gn, sh1, sc1, P['swa_w_qkv'][j].astype(BF16), None, splits=1,
                       n_out=1, epilogue=_ep_rope, tm=tm, tn=QKV_TN, rows_per_seq=rps,
                       out_dtype=F32, extras=tabs, extra_specs=[tspec] * 3, name="swa_qkv")
            dq, dkv = N_HEADS * HEAD_DIM, N_KV * HEAD_DIM
            sinks = P['swa_sinks'][j]
            if prompt:
                o = swa_prompt(qkv, sinks, bsz=bsz, seq=seq)
                last = qkv.reshape(bsz, seq, dq + 2 * dkv)[:, seq - WINDOW:]
                new['k'] = last[..., dq:dq + dkv].reshape(bsz, WINDOW, N_KV, HEAD_DIM)
                new['v'] = last[..., dq + dkv:].reshape(bsz, WINDOW, N_KV, HEAD_DIM)
            else:
                bn = x.shape[0]
                ng = N_HEADS // N_KV
                qg = qkv[:, :dq].reshape(bn, N_KV, ng, HEAD_DIM).transpose(0, 2, 1, 3)
                sinks_gh = sinks.reshape(N_KV, ng).T.reshape(N_HEADS, 1)
                og, ko, vo = swa_decode(qg.reshape(bn, ng, dkv), qkv[:, dq:dq + dkv],
                                        qkv[:, dq + dkv:], st['k'][j].reshape(bn, WINDOW, dkv),
                                        st['v'][j].reshape(bn, WINDOW, dkv), sinks_gh)
                o = og.reshape(bn, ng, N_KV, HEAD_DIM).transpose(0, 2, 1, 3).reshape(bn, dq)
                new['k'] = ko.reshape(bn, WINDOW, N_KV, HEAD_DIM)
                new['v'] = vo.reshape(bn, WINDOW, N_KV, HEAD_DIM)
            x = tail([o], [], P['swa_w_o'][j].astype(BF16), None, x, g1, mlp, prologue=_pro_id,
                     tm=tm, rows_per_seq=rps, name="swa_out")
        elif mixer == 2:
            w_main, w_ba, w_bat = _gdn_weights(P['gdn_w_in'][j])
            proj, ba, bat = _gdn_inproj(x, gn, sh1, sc1, w_main, w_ba, w_bat, tm=tm,
                                        rows_per_seq=rps, out_dtype=act)
            wc, al, dtb = P['gdn_w_conv'][j], P['gdn_a_log'][j], P['gdn_dt_bias'][j]
            if prompt:
                q, k, v, g, beta, gt = gdn_prep_prompt(proj, ba, bat, wc, al, dtb, tt=TT_PROMPT,
                                                       rows_per_seq=rps)
                o, new['ssm'] = gdn_chunked(q, k, v, g, beta, gt, bsz=bsz, seq=seq)
                new['gconv'] = proj.reshape(bsz, seq, -1)[:, seq - (GDN_CONV_K - 1):,
                                                          :GDN_CONV_DIM].astype(F32)
            else:
                y, new['gconv'] = conv_step(st['gconv'][j], proj[:, :GDN_CONV_DIM], wc,
                                            act=_silu, name="gdn_conv_step")
                q, k, v, g, beta = gdn_prep_step(y, ba, al, dtb)
                o, new['ssm'] = gdn_decode(q, k, v, g, beta, st['ssm'][j])
            x = tail([o, (proj, GDN_V_W, GDN_CONV_DIM // GDN_V_W)], [P['gdn_norm'][j]],
                     P['gdn_w_o'][j].astype(BF16), None, x, g1, mlp, prologue=_pro_gdn, tm=tm,
                     rows_per_seq=rps, name="gdn_out")
        else:
            gb, p = nmm(x, gn, sh1, sc1, P['sconv_w_in'][j].astype(BF16), None, splits=3,
                        n_out=2, epilogue=_ep_sconv, tm=tm, tn=512, rows_per_seq=rps,
                        out_dtype=act, name="sconv_in")
            wo = P['sconv_w_out'][j].astype(BF16)
            if prompt:
                x = conv_tail(p, P['sconv_w_conv'][j], [gb], [], wo, None, x, g1, mlp,
                              halo=HALO_MIN,
                              tt=TT_PROMPT, rows_per_seq=rps, prologue=_pro_mul,
                              name="sconv_conv")
                new['sconv'] = p.reshape(bsz, seq, d)[:, seq - (SCONV_K - 1):].astype(F32)
            else:
                y, new['sconv'] = conv_step(st['sconv'][j], p, P['sconv_w_conv'][j],
                                            name="sconv_step")
                x = tail([y, gb], [], wo, None, x, g1, mlp, prologue=_pro_mul, tm=tm,
                         rows_per_seq=rps, name="sconv_tail")
    return x, new


def kernel(x_prompt, x_sample, c_prompt, c_sample, state_conf_conv, cache_swa_k, cache_swa_v,
           state_gdn_ssm, state_gdn_conv, state_sconv, w_ada, b_ada, norm_mix, norm_mlp,
           w_up, w_down, norm_final, conf_w_pw1, conf_b_pw1, conf_w_dw, conf_b_dw, conf_ln_g,
           conf_ln_b, conf_w_pw2, conf_b_pw2, swa_w_qkv, swa_w_o, swa_sinks, gdn_w_in,
           gdn_w_conv, gdn_a_log, gdn_dt_bias, gdn_norm, gdn_w_o, sconv_w_in, sconv_w_conv,
           sconv_w_out):
    P = dict(norm_mix=norm_mix, norm_mlp=norm_mlp, w_up=w_up, w_down=w_down,
             norm_final=norm_final, conf_w_pw1=conf_w_pw1, conf_b_pw1=conf_b_pw1,
             conf_w_dw=conf_w_dw, conf_b_dw=conf_b_dw, conf_ln_g=conf_ln_g, conf_ln_b=conf_ln_b,
             conf_w_pw2=conf_w_pw2, conf_b_pw2=conf_b_pw2, swa_w_qkv=swa_w_qkv, swa_w_o=swa_w_o,
             swa_sinks=swa_sinks, gdn_w_in=gdn_w_in, gdn_w_conv=gdn_w_conv,
             gdn_a_log=gdn_a_log, gdn_dt_bias=gdn_dt_bias, gdn_norm=gdn_norm, gdn_w_o=gdn_w_o,
             sconv_w_in=sconv_w_in, sconv_w_conv=sconv_w_conv, sconv_w_out=sconv_w_out)
    bsz, seq, d = x_prompt.shape
    bn = x_sample.shape[0]
    n_c = bsz + bn
    pad = (-n_c) % ADA_ROWS_PAD
    c_all = jnp.concatenate([c_prompt, c_sample, jnp.zeros((pad, d), F32)], axis=0)
    mods = ada_all(c_all, w_ada, b_ada)
    st = dict(conf=state_conf_conv, k=cache_swa_k, v=cache_swa_v, ssm=state_gdn_ssm,
              gconv=state_gdn_conv, sconv=state_sconv)
    yp, sp = _trunk(x_prompt.reshape(bsz * seq, d),
                    lambda i: _layer_mods(mods, i, 0, bsz, True), True, bsz, seq, None, P)
    ys, ss = _trunk(x_sample.reshape(bn, d),
                    lambda i: _layer_mods(mods, i, bsz, bsz + bn, False), False, bn, 1, st, P)
    names = ('conf', 'k', 'v', 'ssm', 'gconv', 'sconv')
    outs = [yp.reshape(bsz, seq, d), ys.reshape(bn, 1, d)]
    for nm in names:
        outs += [sp[nm][None], ss[nm][None]]
    return tuple(outs)
```
